```python
import functools
import jax, jax.numpy as jnp
from jax import lax
import numpy as np

D_MODEL = 1024
BATCH = 16
SEQ = 2048
DEPTH = 1
DEC_BATCH = 128
DEC_SEQ = 4
PAST_LEN = 8192
PAGE_SIZE = 128

HEAD_DIM = 64
D_RW = D_MODEL // 2
H_RW = D_RW // HEAD_DIM
D_FX = D_MODEL - D_RW
H_FX = D_FX // HEAD_DIM
R_DECAY = 64
R_ICL = 64
R_GATE = 128
C_RW = 3 * D_RW + R_DECAY + R_ICL + R_GATE
C_FX = 3 * D_FX + H_FX
C_IN = C_RW + C_FX
D_FF = 2816
CONV_W = 3
PLE_DIM = 256
Q_BLOCK = 128
LN_EPS = 1e-5
GN_EPS = 64e-5
RMS_EPS = 1e-6
SCALE = HEAD_DIM ** -0.5
ALPHA = (2.0 * DEPTH) ** 0.25
BETA = (8.0 * DEPTH) ** -0.25
FORGET_BIAS_LO = 3.0
FORGET_BIAS_HI = 9.0

kernel_name = 'hymba_rwkv7_fox_convffn_step'


def layer_norm(x, g, b):
    xf = x.astype(jnp.float32)
    mu = jnp.mean(xf, -1, keepdims=True)
    var = jnp.mean(jnp.square(xf - mu), -1, keepdims=True)
    return ((xf - mu) * lax.rsqrt(var + LN_EPS) * g + b).astype(x.dtype)


def rwkv7_group(zr, shift_prev, wkv_prev, lw):
    B, T, _ = zr.shape
    prev = jnp.concatenate([shift_prev[:, None].astype(zr.dtype), zr[:, :-1]], axis=1)
    zs = zr + (prev - zr) * lw['rw_mu']
    o = 3 * D_RW
    r, k, v = zs[..., :D_RW], zs[..., D_RW:2 * D_RW], zs[..., 2 * D_RW:o]
    xw = zs[..., o:o + R_DECAY]
    xa = zs[..., o + R_DECAY:o + R_DECAY + R_ICL]
    xg = zs[..., o + R_DECAY + R_ICL:]
    w_log = -jax.nn.softplus(-(lw['rw_w0'] + jnp.tanh(xw) @ lw['rw_w2'])) - 0.5
    a = jax.nn.sigmoid(lw['rw_a0'] + xa @ lw['rw_a2'])
    g = jax.nn.sigmoid(xg) @ lw['rw_g2']
    heads = lambda t: t.reshape(B, T, H_RW, HEAD_DIM).astype(jnp.float32)
    kk = heads(k * lw['rw_kk'])
    kk = kk / jnp.maximum(jnp.sqrt(jnp.sum(kk * kk, -1, keepdims=True)), 1e-12)
    k_h = heads(k * (1 + (a - 1) * lw['rw_ka']))
    r_h, v_h, a_h = heads(r), heads(v), heads(a)
    decay = jnp.exp(-jnp.exp(heads(w_log)))

    def step(S, inp):
        r_t, d_t, k_t, v_t, kk_t, a_t = inp
        sa = jnp.einsum('bhij,bhj->bhi', S, -kk_t)
        S = (S * d_t[:, :, None, :] + sa[..., None] * (kk_t * a_t)[:, :, None, :]
             + v_t[..., None] * k_t[:, :, None, :])
        return S, jnp.einsum('bhij,bhj->bhi', S, r_t)

    xs = tuple(jnp.moveaxis(t, 1, 0) for t in (r_h, decay, k_h, v_h, kk, a_h))
    S_fin, out = lax.scan(step, wkv_prev.astype(jnp.float32), xs)
    out = jnp.moveaxis(out, 0, 1)
    mu = jnp.mean(out, -1, keepdims=True)
    var = jnp.mean(jnp.square(out - mu), -1, keepdims=True)
    out_n = ((out - mu) * lax.rsqrt(var + GN_EPS)).reshape(B, T, D_RW) * lw['rw_gn_g'] + lw['rw_gn_b']
    bonus = jnp.sum(r_h * k_h * lw['rw_rk'], -1, keepdims=True) * v_h
    y = (out_n + bonus.reshape(B, T, D_RW)) * g
    return y.astype(zr.dtype), zr[:, -1], S_fin


def fox_prompt(q, k, v, logf):
    B, S, H, _ = q.shape
    nb = S // Q_BLOCK
    F = jnp.cumsum(logf, axis=1).transpose(0, 2, 1)
    qb = jnp.moveaxis(q.reshape(B, nb, Q_BLOCK, H, HEAD_DIM), 1, 0)
    Fq = jnp.moveaxis(F.reshape(B, H, nb, Q_BLOCK), 2, 0)
    vf = v.astype(jnp.float32)
    kpos = jnp.arange(S)

    def block(args):
        qi, Fi, start = args
        s = jnp.einsum('bqhd,bkhd->bhqk', qi, k, preferred_element_type=jnp.float32) * SCALE
        s = s + Fi[..., None] - F[:, :, None, :]
        qpos = start + jnp.arange(Q_BLOCK)
        s = jnp.where(kpos[None, :] <= qpos[:, None], s, -jnp.inf)
        p = jax.nn.softmax(s, axis=-1)
        return jnp.einsum('bhqk,bkhd->bqhd', p, vf)

    out = lax.map(block, (qb, Fq, jnp.arange(nb) * Q_BLOCK))
    return jnp.moveaxis(out, 0, 1).reshape(B, S, H, HEAD_DIM)


def fox_sample(q, k_new, v_new, logf_new, cache_k, cache_v, cache_logf, page_table):
    Bd, n_pages = page_table.shape
    lf = cache_logf[page_table].astype(jnp.float32).reshape(Bd, n_pages * PAGE_SIZE, H_FX)
    R = jnp.flip(jnp.cumsum(jnp.flip(lf, 1), 1), 1) - lf
    Rp = R.reshape(Bd, n_pages, PAGE_SIZE, H_FX).transpose(1, 0, 3, 2)
    Fq = jnp.cumsum(logf_new, 1).transpose(0, 2, 1)
    T = q.shape[1]

    def page_step(carry, xs):
        m, l, acc = carry
        pid, r_pg = xs
        kp = cache_k[pid]
        vp = cache_v[pid].astype(jnp.float32)
        s = (jnp.einsum('bqhd,bkhd->bhqk', q, kp, preferred_element_type=jnp.float32) * SCALE
             + Fq[..., None] + r_pg[:, :, None, :])
        m_new = jnp.maximum(m, jnp.max(s, -1))
        corr = jnp.exp(m - m_new)
        p = jnp.exp(s - m_new[..., None])
        l = l * corr + jnp.sum(p, -1)
        acc = acc * corr[..., None] + jnp.einsum('bhqk,bkhd->bhqd', p, vp)
        return (m_new, l, acc), None

    init = (jnp.full((Bd, H_FX, T), -jnp.inf, jnp.float32),
            jnp.zeros((Bd, H_FX, T), jnp.float32),
            jnp.zeros((Bd, H_FX, T, HEAD_DIM), jnp.float32))
    (m, l, acc), _ = lax.scan(page_step, init, (page_table.T, Rp))
    s_new = (jnp.einsum('bqhd,bkhd->bhqk', q, k_new, preferred_element_type=jnp.float32) * SCALE
             + Fq[..., None] - Fq[:, :, None, :])
    s_new = jnp.where(jnp.tril(jnp.ones((T, T), bool)), s_new, -jnp.inf)
    m_fin = jnp.maximum(m, jnp.max(s_new, -1))
    corr = jnp.exp(m - m_fin)
    p = jnp.exp(s_new - m_fin[..., None])
    l = l * corr + jnp.sum(p, -1)
    acc = acc * corr[..., None] + jnp.einsum('bhqk,bkhd->bhqd', p, v_new.astype(jnp.float32))
    return (acc / l[..., None]).transpose(0, 2, 1, 3)


def conv_ffn(x, conv_prev, lw):
    u = x @ lw['w_up']
    a, b = u[..., :D_FF], u[..., D_FF:]
    T = a.shape[1]
    a_ext = jnp.concatenate([conv_prev.astype(a.dtype), a], axis=1)
    c = lw['conv_b'] + sum(a_ext[:, j:j + T] * lw['conv_w'][j] for j in range(CONV_W))
    y = (jax.nn.gelu(c, approximate=False) * b) @ lw['w_down']
    return y, a_ext[:, -(CONV_W - 1):]


def trunk_layer(x, pe, shift_prev, wkv_prev, conv_prev, lw, attend):
    B, T, _ = x.shape
    z = x @ lw['w_in']
    rw_y, shift_new, wkv_new = rwkv7_group(z[..., :C_RW], shift_prev, wkv_prev, lw)
    zf = z[..., C_RW:]
    q = zf[..., :D_FX].reshape(B, T, H_FX, HEAD_DIM)
    k = zf[..., D_FX:2 * D_FX].reshape(B, T, H_FX, HEAD_DIM)
    v = zf[..., 2 * D_FX:3 * D_FX].reshape(B, T, H_FX, HEAD_DIM)
    logf = jax.nn.log_sigmoid((zf[..., 3 * D_FX:] + lw['b_f']).astype(jnp.float32))
    att = attend(q, k, v, logf)
    att = att * lax.rsqrt(jnp.mean(att * att, -1, keepdims=True) + RMS_EPS)
    fx_y = (att.reshape(B, T, D_FX) * lw['fx_norm_g']).astype(x.dtype)
    mix = jnp.concatenate([rw_y, fx_y], axis=-1) @ lw['w_o']
    x = layer_norm(ALPHA * x + mix, lw['ln1_g'], lw['ln1_b'])
    ff, conv_new = conv_ffn(x, conv_prev, lw)
    x = layer_norm(ALPHA * x + ff, lw['ln2_g'], lw['ln2_b'])
    x = x + jax.nn.sigmoid(x @ lw['w_pg'] + lw['b_pg']) * (pe @ lw['w_ple'])
    return x, (k, v, logf, wkv_new, shift_new, conv_new)


def setup_inputs(seed: int = 0) -> dict:
    key = jax.random.key(seed)
    ks = iter(jax.random.split(key, 48))
    nrm = lambda shape, s=1.0: jax.random.normal(next(ks), shape, jnp.float32) * s
    n_pages = PAST_LEN // PAGE_SIZE
    n_used = DEC_BATCH * n_pages
    n_pool = n_used + max(1, n_used // 4)
    L = DEPTH
    head_bias = jnp.linspace(FORGET_BIAS_LO, FORGET_BIAS_HI, H_FX, dtype=jnp.float32)
    d = {}
    d['x_prompt'] = nrm((BATCH, SEQ, D_MODEL))
    d['x_sample'] = nrm((DEC_BATCH, DEC_SEQ, D_MODEL))
    d['cache_k'] = nrm((L, n_pool, PAGE_SIZE, H_FX, HEAD_DIM))
    d['cache_v'] = nrm((L, n_pool, PAGE_SIZE, H_FX, HEAD_DIM))
    d['cache_logf'] = jax.nn.log_sigmoid(head_bias + nrm((L, n_pool, PAGE_SIZE, H_FX)))
    d['state_wkv'] = nrm((L, DEC_BATCH, H_RW, HEAD_DIM, HEAD_DIM), 0.5)
    d['state_shift'] = nrm((L, DEC_BATCH, C_RW))
    d['state_conv'] = nrm((L, DEC_BATCH, CONV_W - 1, D_FF))
    perm = jax.random.permutation(next(ks), n_pool)[:n_used]
    d['page_table'] = perm.reshape(DEC_BATCH, n_pages).astype(jnp.int32)
    d['p_prompt'] = nrm((L, BATCH, SEQ, PLE_DIM))
    d['p_sample'] = nrm((L, DEC_BATCH, DEC_SEQ, PLE_DIM))
    d['ln_in_g'] = 1.0 + nrm((D_MODEL,), 0.02)
    d['ln_in_b'] = nrm((D_MODEL,), 0.02)
    d['w_in'] = nrm((L, D_MODEL, C_IN), D_MODEL ** -0.5)
    d['b_f'] = head_bias + nrm((L, H_FX), 0.1)
    d['rw_mu'] = jax.random.uniform(next(ks), (L, C_RW), jnp.float32)
    d['rw_w0'] = nrm((L, D_RW), 0.5)
    d['rw_w2'] = nrm((L, R_DECAY, D_RW), R_DECAY ** -0.5)
    d['rw_a0'] = nrm((L, D_RW), 0.1)
    d['rw_a2'] = nrm((L, R_ICL, D_RW), R_ICL ** -0.5)
    d['rw_g2'] = nrm((L, R_GATE, D_RW), R_GATE ** -0.5)
    d['rw_kk'] = 0.85 + nrm((L, D_RW), 0.05)
    d['rw_ka'] = 1.0 + nrm((L, D_RW), 0.05)
    d['rw_rk'] = nrm((L, H_RW, HEAD_DIM), 0.1)
    d['rw_gn_g'] = 1.0 + nrm((L, D_RW), 0.02)
    d['rw_gn_b'] = nrm((L, D_RW), 0.02)
    d['fx_norm_g'] = 1.0 + nrm((L, D_FX), 0.02)
    d['w_o'] = nrm((L, D_MODEL, D_MODEL), BETA * D_MODEL ** -0.5)
    d['ln1_g'] = 1.0 + nrm((L, D_MODEL), 0.02)
    d['ln1_b'] = nrm((L, D_MODEL), 0.02)
    d['w_up'] = nrm((L, D_MODEL, 2 * D_FF), D_MODEL ** -0.5)
    d['conv_w'] = nrm((L, CONV_W, D_FF), CONV_W ** -0.5)
    d['conv_b'] = nrm((L, D_FF), 0.02)
    d['w_down'] = nrm((L, D_FF, D_MODEL), BETA * D_FF ** -0.5)
    d['ln2_g'] = 1.0 + nrm((L, D_MODEL), 0.02)
    d['ln2_b'] = nrm((L, D_MODEL), 0.02)
    d['w_ple'] = nrm((L, PLE_DIM, D_MODEL), PLE_DIM ** -0.5)
    d['w_pg'] = nrm((L, D_MODEL, D_MODEL), D_MODEL ** -0.5)
    d['b_pg'] = nrm((L, D_MODEL), 0.02)
    return d


def reference(x_prompt, x_sample, cache_k, cache_v, cache_logf, state_wkv, state_shift, state_conv,
              page_table, p_prompt, p_sample, ln_in_g, ln_in_b, w_in, b_f, rw_mu, rw_w0, rw_w2, rw_a0,
              rw_a2, rw_g2, rw_kk, rw_ka, rw_rk, rw_gn_g, rw_gn_b, fx_norm_g, w_o, ln1_g, ln1_b, w_up,
              conv_w, conv_b, w_down, ln2_g, ln2_b, w_ple, w_pg, b_pg):
    xp = layer_norm(x_prompt, ln_in_g, ln_in_b)
    xs = layer_norm(x_sample, ln_in_g, ln_in_b)
    Bp = xp.shape[0]
    outs_p, outs_s = [], []
    for i in range(DEPTH):
        lw = dict(w_in=w_in[i], b_f=b_f[i], rw_mu=rw_mu[i], rw_w0=rw_w0[i], rw_w2=rw_w2[i],
                  rw_a0=rw_a0[i], rw_a2=rw_a2[i], rw_g2=rw_g2[i], rw_kk=rw_kk[i], rw_ka=rw_ka[i],
                  rw_rk=rw_rk[i], rw_gn_g=rw_gn_g[i], rw_gn_b=rw_gn_b[i], fx_norm_g=fx_norm_g[i],
                  w_o=w_o[i], ln1_g=ln1_g[i], ln1_b=ln1_b[i], w_up=w_up[i], conv_w=conv_w[i],
                  conv_b=conv_b[i], w_down=w_down[i], ln2_g=ln2_g[i], ln2_b=ln2_b[i],
                  w_ple=w_ple[i], w_pg=w_pg[i], b_pg=b_pg[i])
        xp, st_p = trunk_layer(xp, p_prompt[i],
                               jnp.zeros((Bp, C_RW), xp.dtype),
                               jnp.zeros((Bp, H_RW, HEAD_DIM, HEAD_DIM), jnp.float32),
                               jnp.zeros((Bp, CONV_W - 1, D_FF), xp.dtype),
                               lw, fox_prompt)
        attend_s = functools.partial(fox_sample, cache_k=cache_k[i], cache_v=cache_v[i],
                                     cache_logf=cache_logf[i], page_table=page_table)
        xs, st_s = trunk_layer(xs, p_sample[i], state_shift[i], state_wkv[i], state_conv[i], lw, attend_s)
        outs_p.append(st_p)
        outs_s.append(st_s)
    stk = lambda outs, j: jnp.stack([o[j] for o in outs])
    return (xp, xs,
            stk(outs_p, 0), stk(outs_p, 1), stk(outs_p, 2), stk(outs_p, 3), stk(outs_p, 4), stk(outs_p, 5),
            stk(outs_s, 0), stk(outs_s, 1), stk(outs_s, 2), stk(outs_s, 3), stk(outs_s, 4), stk(outs_s, 5))
```

```python
import functools

import jax
import jax.numpy as jnp
from jax import lax
from jax.experimental import pallas as pl
from jax.experimental.pallas import tpu as pltpu

D_MODEL = 1024
HEAD_DIM = 64
D_RW = 512
H_RW = 8
D_FX = 512
H_FX = 8
R_DECAY = 64
R_ICL = 64
R_GATE = 128
C_RW = 3 * D_RW + R_DECAY + R_ICL + R_GATE
D_FF = 2816
CONV_W = 3
PLE_DIM = 256
PAGE_SIZE = 128
LN_EPS = 1e-5
GN_EPS = 64e-5
RMS_EPS = 1e-6
SCALE = HEAD_DIM ** -0.5
ALPHA = 2.0 ** 0.25

LANES = 128
SUBLANES = 8
NEG_BIG = -1e30
VMEM_LIMIT = 56 * 1024 * 1024

BF16 = jnp.bfloat16
F32 = jnp.float32


def _dot(a, b):
    return jnp.dot(a, b, preferred_element_type=F32)


def _dot_nt(a, b):
    return lax.dot_general(a, b, (((1,), (1,)), ((), ())), preferred_element_type=F32)


def _split3(x):
    hi = x.astype(BF16).astype(F32)
    r1 = x - hi
    mid = r1.astype(BF16).astype(F32)
    lo = (r1 - mid).astype(BF16).astype(F32)
    return hi, mid, lo


def _split2(x):
    hi = x.astype(BF16)
    lo = (x - hi.astype(F32)).astype(BF16)
    return hi, lo


def _segsum(x, bd):
    hi, lo = _split2(x)
    return _dot(hi, bd) + _dot(lo, bd)


def _layer_norm(x, g, b):
    mu = jnp.mean(x, -1, keepdims=True)
    xc = x - mu
    var = jnp.mean(xc * xc, -1, keepdims=True)
    return xc * lax.rsqrt(var + LN_EPS) * g + b


def _softplus(y):
    return jnp.maximum(y, 0.0) + jnp.log1p(jnp.exp(-jnp.abs(y)))


def _const_spec(shape):
    nd = len(shape)
    return pl.BlockSpec(shape, lambda *_: (0,) * nd, pipeline_mode=pl.Buffered(1))


def _params(sem):
    return pltpu.CompilerParams(dimension_semantics=sem, vmem_limit_bytes=VMEM_LIMIT)


def _proj_kernel(*refs, tm, sh, hist, with_cumsum):
    (x_ref, hist0_ref, lng, lnb, w_rw, w_qkv, w_f, b_f, mu, w0, w2, a0, a2, g2, kkw, kaw, rk,
     bd) = refs[:18]
    pos = 18
    if with_cumsum:
        tri = refs[pos]
        pos += 1
    (r_o, d_o, k_o, v_o, nkk_o, ka_o, g_o, bonus_o, q_o, kf_o, vf_o, logf_o) = refs[pos:pos + 12]
    pos += 12
    if with_cumsum:
        cum_o = refs[pos]
        pos += 1
    shift_o = refs[pos]
    zext = refs[pos + 1]
    if with_cumsum:
        fcarry = refs[pos + 2]

    t = pl.program_id(1)

    @pl.when(t == 0)
    def _():
        zext[0:hist, :] = hist0_ref[0]
        if with_cumsum:
            fcarry[...] = jnp.zeros_like(fcarry)

    xn = _layer_norm(x_ref[0], lng[...], lnb[...])
    xb = xn.astype(BF16)

    zr = _dot(xb, w_rw[...])
    zext[hist:hist + tm, :] = zr
    prev = zext[hist - sh:hist - sh + tm, :]
    zs = zr + (prev - zr) * mu[...]
    new_hist = zext[tm:tm + hist, :]
    shift_o[0] = new_hist
    zext[0:hist, :] = new_hist

    r = zs[:, 0:D_RW]
    k = zs[:, D_RW:2 * D_RW]
    v = zs[:, 2 * D_RW:3 * D_RW]
    lw = zs[:, 3 * D_RW:3 * D_RW + R_DECAY + R_ICL]
    xg = zs[:, 3 * D_RW + R_DECAY + R_ICL:C_RW]
    w_log = -_softplus(-(w0[...] + _dot(jnp.tanh(lw).astype(BF16), w2[...]))) - 0.5
    a = jax.nn.sigmoid(a0[...] + _dot(lw.astype(BF16), a2[...]))
    g = _dot(jax.nn.sigmoid(xg).astype(BF16), g2[...])
    kkr = k * kkw[...]
    kk = kkr / jnp.maximum(jnp.sqrt(_segsum(kkr * kkr, bd[...])), 1e-12)
    k_h = k * (1.0 + (a - 1.0) * kaw[...])
    r_o[0] = r
    d_o[0] = jnp.exp(-jnp.exp(w_log))
    k_o[0] = k_h
    v_o[0] = v
    nkk_o[0] = -kk
    ka_o[0] = kk * a
    g_o[0] = g
    bonus_o[0] = _segsum(r * k_h * rk[...], bd[...]) * v

    qkv = _dot(xb, w_qkv[...])
    q_o[0] = (qkv[:, 0:D_FX] * SCALE).astype(BF16)
    kf_o[0] = qkv[:, D_FX:2 * D_FX]
    vf_o[0] = qkv[:, 2 * D_FX:3 * D_FX]
    fl = _dot(xb, w_f[...]) + b_f[...]
    logf = -_softplus(-fl)
    logf_o[0] = logf[:, 0:H_FX]
    if with_cumsum:
        hi, mid, lo = _split3(logf)
        trib = tri[...]
        cum = (_dot(trib, hi.astype(BF16)) + _dot(trib, mid.astype(BF16))
               + _dot(trib, lo.astype(BF16)) + fcarry[...])
        cum_o[0] = cum[:, 0:H_FX]
        fcarry[...] = cum[tm - 1:tm, :]


def _proj(x, hist0, wts, *, tm, sh, with_cumsum):
    nb, n, _ = x.shape
    hist = hist0.shape[1]
    names = ['ln_in_g', 'ln_in_b', 'w_rw', 'w_qkv', 'w_f', 'b_f', 'rw_mu', 'rw_w0', 'rw_w2', 'rw_a0',
             'rw_a2', 'rw_g2', 'rw_kk', 'rw_ka', 'rw_rk', 'bd']
    consts = [wts[nm] for nm in names]
    if with_cumsum:
        consts.append(jnp.tril(jnp.ones((tm, tm), BF16)))
    row = lambda c, dt=F32: (jax.ShapeDtypeStruct((nb, n, c), dt), pl.BlockSpec((1, tm, c), lambda b, t: (b, t, 0)))
    outs = [row(D_RW) for _ in range(8)] + [row(D_FX, BF16), row(D_FX), row(D_FX), row(H_FX)]
    if with_cumsum:
        outs.append(row(H_FX))
    outs.append((jax.ShapeDtypeStruct((nb, hist, C_RW), F32), pl.BlockSpec((1, hist, C_RW), lambda b, t: (b, 0, 0))))
    scratch = [pltpu.VMEM((hist + tm, C_RW), F32)]
    if with_cumsum:
        scratch.append(pltpu.VMEM((1, LANES), F32))
    return pl.pallas_call(
        functools.partial(_proj_kernel, tm=tm, sh=sh, hist=hist, with_cumsum=with_cumsum),
        grid=(nb, n // tm),
        in_specs=[pl.BlockSpec((1, tm, D_MODEL), lambda b, t: (b, t, 0)),
                  pl.BlockSpec((1, hist, C_RW), lambda b, t: (b, 0, 0))]
                 + [_const_spec(c.shape) for c in consts],
        out_specs=[o[1] for o in outs],
        out_shape=[o[0] for o in outs],
        scratch_shapes=scratch,
        compiler_params=_params(("parallel", "arbitrary")),
        name="proj",
    )(x, hist0, *consts)


def _scan_kernel(r_ref, d_ref, k_ref, nkk_ref, ka_ref, v_ref, s0_ref, out_ref, s_ref, *, tt):
    @pl.when(pl.program_id(1) == 0)
    def _():
        s_ref[...] = s0_ref[...]

    def step(i, carry):
        row = lambda ref, j: ref[i, pl.ds(j, 1), :]
        v_t = v_ref[i]
        sa0 = jnp.zeros((HEAD_DIM, LANES), F32)
        sa1 = jnp.zeros((HEAD_DIM, LANES), F32)
        for j in range(0, HEAD_DIM, 2):
            sa0 = sa0 + s_ref[j] * row(nkk_ref, j)
            sa1 = sa1 + s_ref[j + 1] * row(nkk_ref, j + 1)
        sa = sa0 + sa1
        o0 = jnp.zeros((HEAD_DIM, LANES), F32)
        o1 = jnp.zeros((HEAD_DIM, LANES), F32)
        for j in range(HEAD_DIM):
            sj = s_ref[j] * row(d_ref, j) + sa * row(ka_ref, j) + v_t * row(k_ref, j)
            s_ref[j] = sj
            if j % 2 == 0:
                o0 = o0 + sj * row(r_ref, j)
            else:
                o1 = o1 + sj * row(r_ref, j)
        out_ref[i] = o0 + o1
        return carry

    lax.fori_loop(0, tt, step, 0)


def _scan(xs, s0, *, tt):
    steps, groups = xs[0].shape[:2]
    seq = pl.BlockSpec((tt, None, HEAD_DIM, LANES), lambda g, t: (t, g, 0, 0))
    st = pl.BlockSpec((None, HEAD_DIM, HEAD_DIM, LANES), lambda g, t: (g, 0, 0, 0))
    return pl.pallas_call(
        functools.partial(_scan_kernel, tt=tt),
        grid=(groups, steps // tt),
        in_specs=[seq] * 6 + [st],
        out_specs=[seq, st],
        out_shape=[jax.ShapeDtypeStruct(xs[0].shape, F32), jax.ShapeDtypeStruct(s0.shape, F32)],
        compiler_params=_params(("parallel", "arbitrary")),
        name="scan",
    )(*xs, s0)


def _bias_cols(cols, lane):
    out = jnp.zeros((), F32)
    for i in reversed(range(len(cols))):
        out = jnp.where(lane == i, cols[i], out)
    return out


def _attn_kernel(q_ref, k_ref, v_ref, fq_ref, fk_ref, g_ref, o_ref, qa0, qa1, m0_s, m1_s, l0_s, l1_s, acc,
                 *, tq, tk):
    qi = pl.program_id(2)
    ki = pl.program_id(3)
    lane = lax.broadcasted_iota(jnp.int32, (1, LANES), 1)
    head0 = lane < HEAD_DIM

    @pl.when(ki == 0)
    def _():
        q2 = q_ref[0]
        fq = fq_ref[0]
        p0 = _split3(fq[:, 0:1])
        p1 = _split3(fq[:, 1:2])
        e0 = _bias_cols(list(p0) + [1.0, 1.0, 1.0], lane)
        e1 = _bias_cols([0.0] * 6 + list(p1) + [1.0, 1.0, 1.0], lane)
        zero = jnp.zeros_like(q2)
        qa0[...] = jnp.concatenate([jnp.where(head0, q2, zero), e0.astype(BF16)], axis=1)
        qa1[...] = jnp.concatenate([jnp.where(head0, zero, q2), e1.astype(BF16)], axis=1)
        m0_s[...] = jnp.full_like(m0_s, NEG_BIG)
        m1_s[...] = jnp.full_like(m1_s, NEG_BIG)
        l0_s[...] = jnp.zeros_like(l0_s)
        l1_s[...] = jnp.zeros_like(l1_s)
        acc[...] = jnp.zeros_like(acc)

    @pl.when(ki <= qi)
    def _():
        fk = fk_ref[0]
        n0 = [-x for x in _split3(fk[:, 0:1])]
        n1 = [-x for x in _split3(fk[:, 1:2])]
        ek = _bias_cols([1.0, 1.0, 1.0] + n0 + [1.0, 1.0, 1.0] + n1, lane)
        ka = jnp.concatenate([k_ref[0].astype(BF16), ek.astype(BF16)], axis=1)
        v2 = v_ref[0]
        vz = jnp.zeros_like(v2)
        rows = qi * tq + lax.broadcasted_iota(jnp.int32, (tq, tk), 0)
        cols = ki * tk + lax.broadcasted_iota(jnp.int32, (tq, tk), 1)
        causal = cols <= rows
        corr = []
        pv = []
        for qa, m_s, l_s, vh in ((qa0, m0_s, l0_s, jnp.where(head0, v2, vz)),
                                 (qa1, m1_s, l1_s, jnp.where(head0, vz, v2))):
            s = jnp.where(causal, _dot_nt(qa[...], ka), NEG_BIG)
            m_old = m_s[...]
            m_new = jnp.maximum(m_old, jnp.max(s, -1, keepdims=True))
            c = jnp.exp(m_old - m_new)
            p = jnp.exp(s - m_new)
            l_s[...] = l_s[...] * c + jnp.sum(p, -1, keepdims=True)
            m_s[...] = m_new
            corr.append(c)
            pv.append(_dot(p.astype(BF16), vh.astype(BF16)))
        acc[...] = acc[...] * jnp.where(head0, corr[0], corr[1]) + pv[0] + pv[1]

    @pl.when(ki == qi)
    def _():
        o = acc[...] * jnp.where(head0, 1.0 / l0_s[...], 1.0 / l1_s[...])
        o2 = o * o
        ms0 = jnp.sum(jnp.where(head0, o2, 0.0), -1, keepdims=True) * (1.0 / HEAD_DIM)
        ms1 = jnp.sum(jnp.where(head0, 0.0, o2), -1, keepdims=True) * (1.0 / HEAD_DIM)
        rn = jnp.where(head0, lax.rsqrt(ms0 + RMS_EPS), lax.rsqrt(ms1 + RMS_EPS))
        o_ref[0] = (o * rn * g_ref[...]).astype(BF16)


def _attn(q, k, v, cum_pairs, gain, *, tq):
    nb, s, _ = q.shape
    tk = tq
    pairs = D_FX // LANES
    kv_idx = lambda b, p, qi, ki: (b, jnp.minimum(ki, qi), p)
    return pl.pallas_call(
        functools.partial(_attn_kernel, tq=tq, tk=tk),
        grid=(nb, pairs, s // tq, s // tk),
        in_specs=[pl.BlockSpec((1, tq, LANES), lambda b, p, qi, ki: (b, qi, p)),
                  pl.BlockSpec((1, tk, LANES), kv_idx),
                  pl.BlockSpec((1, tk, LANES), kv_idx),
                  pl.BlockSpec((1, None, tq, 2), lambda b, p, qi, ki: (b, p, qi, 0)),
                  pl.BlockSpec((1, None, tk, 2), lambda b, p, qi, ki: (b, p, jnp.minimum(ki, qi), 0)),
                  pl.BlockSpec((1, LANES), lambda b, p, qi, ki: (0, p))],
        out_specs=pl.BlockSpec((1, tq, LANES), lambda b, p, qi, ki: (b, qi, p)),
        out_shape=jax.ShapeDtypeStruct((nb, s, D_FX), BF16),
        scratch_shapes=[pltpu.VMEM((tq, 2 * LANES), BF16), pltpu.VMEM((tq, 2 * LANES), BF16),
                        pltpu.VMEM((tq, 1), F32), pltpu.VMEM((tq, 1), F32),
                        pltpu.VMEM((tq, 1), F32), pltpu.VMEM((tq, 1), F32),
                        pltpu.VMEM((tq, LANES), F32)],
        compiler_params=_params(("parallel", "parallel", "parallel", "arbitrary")),
        name="attn",
    )(q, k, v, cum_pairs, cum_pairs, gain)


def _dec_attn_kernel(pt_ref, q_ref, kn_ref, vn_ref, lfn_ref, gt_ref, hm_ref, g_ref, *rest, pc, nc, t_new):
    kp = rest[0:pc]
    vp = rest[pc:2 * pc]
    lp = rest[2 * pc:3 * pc]
    o_ref, m_s, l_s, acc, carry = rest[3 * pc:]
    c = pl.program_id(1)
    q = q_ref[0]
    gt = gt_ref[...]

    def later_sum(lf):
        hi, mid, lo = _split3(lf)
        stacked = jnp.concatenate([hi, mid, lo, jnp.zeros_like(hi)], axis=0).astype(BF16)
        r = _dot(stacked, gt)
        return r[0:H_FX] + r[H_FX:2 * H_FX] + r[2 * H_FX:3 * H_FX]

    def update(s, vb):
        m_old = m_s[...]
        m_new = jnp.maximum(m_old, jnp.max(s, -1, keepdims=True))
        corr = jnp.exp(m_old - m_new)
        p = jnp.exp(s - m_new)
        l_s[...] = l_s[...] * corr + jnp.sum(p, -1, keepdims=True)
        m_s[...] = m_new
        acc[...] = acc[...] * corr + _dot(p.astype(BF16), vb)

    @pl.when(c == 0)
    def _():
        m_s[...] = jnp.full_like(m_s, NEG_BIG)
        l_s[...] = jnp.zeros_like(l_s)
        acc[...] = jnp.zeros_like(acc)
        lf = lfn_ref[0]
        nk = kn_ref.shape[1]
        bias = later_sum(lf)[:, 0:nk]
        bias = jnp.concatenate([bias] * t_new, axis=0)
        s = _dot_nt(q, kn_ref[0].astype(BF16)) + bias
        key_t = lax.broadcasted_iota(jnp.int32, s.shape, 1)
        qry_t = lax.broadcasted_iota(jnp.int32, s.shape, 0) // H_FX
        update(jnp.where(key_t <= qry_t, s, NEG_BIG), vn_ref[0].astype(BF16))
        carry[...] = jnp.sum(lf, -1, keepdims=True)

    for i in range(pc):
        lf = lp[i][...]
        bias = later_sum(lf) + carry[...]
        carry[...] = carry[...] + jnp.sum(lf, -1, keepdims=True)
        s = _dot_nt(q, kp[i][...].astype(BF16)) + jnp.concatenate([bias] * t_new, axis=0)
        update(s, vp[i][...].astype(BF16))

    @pl.when(c == nc - 1)
    def _():
        o = acc[...] * (1.0 / l_s[...]) * hm_ref[...]
        ms = jnp.sum(o * o, -1, keepdims=True) * (1.0 / HEAD_DIM)
        o = o * lax.rsqrt(ms + RMS_EPS)
        per_t = [jnp.sum(o[t * H_FX:(t + 1) * H_FX], axis=0, keepdims=True) for t in range(t_new)]
        o_ref[0] = (jnp.concatenate(per_t, axis=0) * g_ref[...]).astype(BF16)


def _dec_attn(page_table, qbig, k_new, v_new, lf_new, cache_k, cache_v, cache_lft, head_mask, gain, *, pc, t_new):
    nb, n_pages = page_table.shape
    nc = n_pages // pc
    nk = k_new.shape[1]
    gt = jnp.triu(jnp.ones((PAGE_SIZE, PAGE_SIZE), BF16), 1).T
    rows = t_new * H_FX

    def page_spec(shape, i):
        return pl.BlockSpec((None,) + shape, lambda b, c, pt: (pt[b, n_pages - 1 - (c * pc + i)], 0, 0))

    per_b = lambda shape: pl.BlockSpec((1,) + shape, lambda b, c, pt: (b, 0, 0))
    const = lambda shape: pl.BlockSpec(shape, lambda b, c, pt: (0, 0))
    in_specs = ([per_b((rows, D_FX)), per_b((nk, D_FX)), per_b((nk, D_FX)), per_b((H_FX, LANES)),
                 const((PAGE_SIZE, PAGE_SIZE)), const((rows, D_FX)), const((1, D_FX))]
                + [page_spec((PAGE_SIZE, D_FX), i) for i in range(pc)]
                + [page_spec((PAGE_SIZE, D_FX), i) for i in range(pc)]
                + [page_spec((H_FX, PAGE_SIZE), i) for i in range(pc)])
    return pl.pallas_call(
        functools.partial(_dec_attn_kernel, pc=pc, nc=nc, t_new=t_new),
        grid_spec=pltpu.PrefetchScalarGridSpec(
            num_scalar_prefetch=1,
            grid=(nb, nc),
            in_specs=in_specs,
            out_specs=pl.BlockSpec((1, t_new, D_FX), lambda b, c, pt: (b, 0, 0)),
            scratch_shapes=[pltpu.VMEM((rows, 1), F32), pltpu.VMEM((rows, 1), F32),
                            pltpu.VMEM((rows, D_FX), F32), pltpu.VMEM((H_FX, 1), F32)]),
        out_shape=jax.ShapeDtypeStruct((nb, t_new, D_FX), BF16),
        compiler_params=_params(("parallel", "arbitrary")),
        name="dec_attn",
    )(page_table, qbig, k_new, v_new, lf_new, gt, head_mask, gain,
      *([cache_k] * pc), *([cache_v] * pc), *([cache_lft] * pc))


def _mix_kernel(x_ref, wkv_ref, bonus_ref, g_ref, fx_ref, lng, lnb, gng, gnb, bd, wo_rw, wo_fx, l1g, l1b, o_ref):
    xn = _layer_norm(x_ref[0], lng[...], lnb[...])
    wkv = wkv_ref[0]
    mean = _segsum(wkv, bd[...]) * (1.0 / HEAD_DIM)
    wc = wkv - mean
    var = _segsum(wc * wc, bd[...]) * (1.0 / HEAD_DIM)
    wn = wc * lax.rsqrt(var + GN_EPS) * gng[...] + gnb[...]
    rw_y = (wn + bonus_ref[0]) * g_ref[0]
    mix = _dot(rw_y.astype(BF16), wo_rw[...]) + _dot(fx_ref[0], wo_fx[...])
    o_ref[0] = _layer_norm(ALPHA * xn + mix, l1g[...], l1b[...])


def _mix(x, wkv, bonus, g, fx, wts, *, tm):
    nb, n, _ = x.shape
    names = ['ln_in_g', 'ln_in_b', 'rw_gn_g', 'rw_gn_b', 'bd', 'wo_rw', 'wo_fx', 'ln1_g', 'ln1_b']
    consts = [wts[nm] for nm in names]
    row = lambda c: pl.BlockSpec((1, tm, c), lambda b, t: (b, t, 0))
    return pl.pallas_call(
        _mix_kernel,
        grid=(nb, n // tm),
        in_specs=[row(D_MODEL), row(D_RW), row(D_RW), row(D_RW), row(D_FX)] + [_const_spec(c.shape) for c in consts],
        out_specs=row(D_MODEL),
        out_shape=jax.ShapeDtypeStruct((nb, n, D_MODEL), F32),
        compiler_params=_params(("parallel", "parallel")),
        name="mix",
    )(x, wkv, bonus, g, fx, *consts)


FF_CHUNK = D_FF // 2


def _ffn_kernel(x_ref, pe_ref, hist0_ref, wa, wb, cw, cb, wd, l2g, l2b, wpg, bpg, wple, y_ref, hist_o, aext,
                *, tm, sh, hist):
    @pl.when(pl.program_id(1) == 0)
    def _():
        aext[0:hist, :] = hist0_ref[0]

    x1 = x_ref[0]
    xb = x1.astype(BF16)
    y = jnp.zeros((tm, D_MODEL), F32)
    for c0 in range(0, D_FF, FF_CHUNK):
        cs = slice(c0, c0 + FF_CHUNK)
        a = _dot(xb, wa[:, cs])
        aext[hist:hist + tm, cs] = a
        p1 = aext[hist - sh:hist - sh + tm, cs]
        p2 = aext[hist - 2 * sh:hist - 2 * sh + tm, cs]
        conv = cb[:, cs] + p2 * cw[0:1, cs] + p1 * cw[1:2, cs] + a * cw[2:3, cs]
        gelu = 0.5 * conv * (1.0 + lax.erf(conv * (2.0 ** -0.5)))
        h = gelu * _dot(xb, wb[:, cs])
        y = y + _dot(h.astype(BF16), wd[cs, :])
    new_hist = aext[tm:tm + hist, :]
    hist_o[0] = new_hist
    aext[0:hist, :] = new_hist
    x2 = _layer_norm(ALPHA * x1 + y, l2g[...], l2b[...])
    gate = jax.nn.sigmoid(_dot(x2.astype(BF16), wpg[...]) + bpg[...])
    y_ref[0] = x2 + gate * _dot(pe_ref[0].astype(BF16), wple[...])


def _ffn(x1, pe, hist0, wts, *, tm, sh):
    nb, n, _ = x1.shape
    hist = hist0.shape[1]
    names = ['w_up_a', 'w_up_b', 'conv_w', 'conv_b', 'w_down', 'ln2_g', 'ln2_b', 'w_pg', 'b_pg', 'w_ple']
    consts = [wts[nm] for nm in names]
    row = lambda c: pl.BlockSpec((1, tm, c), lambda b, t: (b, t, 0))
    hist_spec = pl.BlockSpec((1, hist, D_FF), lambda b, t: (b, 0, 0))
    return pl.pallas_call(
        functools.partial(_ffn_kernel, tm=tm, sh=sh, hist=hist),
        grid=(nb, n // tm),
        in_specs=[row(D_MODEL), row(PLE_DIM), hist_spec] + [_const_spec(c.shape) for c in consts],
        out_specs=[row(D_MODEL), hist_spec],
        out_shape=[jax.ShapeDtypeStruct((nb, n, D_MODEL), F32), jax.ShapeDtypeStruct((nb, hist, D_FF), F32)],
        scratch_shapes=[pltpu.VMEM((hist + tm, D_FF), F32)],
        compiler_params=_params(("parallel", "arbitrary")),
        name="ffn",
    )(x1, pe, hist0, *consts)


def _prep_weights(ln_in_g, ln_in_b, w_in, b_f, rw_mu, rw_w0, rw_w2, rw_a0, rw_a2, rw_g2, rw_kk, rw_ka, rw_rk,
                  rw_gn_g, rw_gn_b, fx_norm_g, w_o, ln1_g, ln1_b, w_up, conv_w, conv_b, w_down, ln2_g, ln2_b,
                  w_ple, w_pg, b_pg):
    r2 = lambda a: a.reshape(1, -1)
    bf = lambda a: a.astype(BF16)
    lora_rows = R_DECAY + R_ICL
    seg = jnp.arange(D_RW) // HEAD_DIM
    return dict(
        ln_in_g=r2(ln_in_g), ln_in_b=r2(ln_in_b),
        w_rw=bf(w_in[:, :C_RW]), w_qkv=bf(w_in[:, C_RW:C_RW + 3 * D_FX]),
        w_f=bf(jnp.pad(w_in[:, C_RW + 3 * D_FX:], ((0, 0), (0, LANES - H_FX)))),
        b_f=jnp.pad(b_f, (0, LANES - H_FX)).reshape(1, LANES),
        rw_mu=r2(rw_mu), rw_w0=r2(rw_w0), rw_a0=r2(rw_a0),
        rw_w2=bf(jnp.pad(rw_w2, ((0, lora_rows - R_DECAY), (0, 0)))),
        rw_a2=bf(jnp.pad(rw_a2, ((lora_rows - R_ICL, 0), (0, 0)))),
        rw_g2=bf(rw_g2), rw_kk=r2(rw_kk), rw_ka=r2(rw_ka), rw_rk=r2(rw_rk),
        bd=(seg[:, None] == seg[None, :]).astype(BF16),
        rw_gn_g=r2(rw_gn_g), rw_gn_b=r2(rw_gn_b), fx_norm_g=r2(fx_norm_g),
        wo_rw=bf(w_o[:D_RW]), wo_fx=bf(w_o[D_RW:]), ln1_g=r2(ln1_g), ln1_b=r2(ln1_b),
        w_up_a=bf(w_up[:, :D_FF]), w_up_b=bf(w_up[:, D_FF:]), conv_w=conv_w, conv_b=r2(conv_b),
        w_down=bf(w_down), ln2_g=r2(ln2_g), ln2_b=r2(ln2_b), w_ple=bf(w_ple), w_pg=bf(w_pg), b_pg=r2(b_pg))


def _prompt_layer(x, pe, wts):
    nb, s, _ = x.shape
    hist = SUBLANES
    (r, d, k, v, nkk, ka, g, bonus, q, kf, vf, logf, cum, shift_hist) = _proj(
        x, jnp.zeros((nb, hist, C_RW), F32), wts, tm=256, sh=1, with_cumsum=True)
    to_scan = lambda a: a.reshape(nb, s, H_RW, HEAD_DIM).transpose(1, 3, 0, 2).reshape(s, 1, HEAD_DIM, nb * H_RW)
    wkv, s_fin = _scan([to_scan(a) for a in (r, d, k, nkk, ka, v)],
                       jnp.zeros((1, HEAD_DIM, HEAD_DIM, nb * H_RW), F32), tt=16)
    wkv = wkv.reshape(s, HEAD_DIM, nb, H_RW).transpose(2, 0, 3, 1).reshape(nb, s, D_RW)
    s_fin = s_fin.reshape(HEAD_DIM, HEAD_DIM, nb, H_RW).transpose(2, 3, 1, 0)
    cum_pairs = cum.reshape(nb, s, H_FX // 2, 2).transpose(0, 2, 1, 3)
    fx = _attn(q, kf, vf, cum_pairs, wts['fx_norm_g'], tq=512)
    x1 = _mix(x, wkv, bonus, g, fx, wts, tm=512)
    y, conv_hist = _ffn(x1, pe, jnp.zeros((nb, hist, D_FF), F32), wts, tm=256, sh=1)
    heads = lambda a: a.reshape(nb, s, H_FX, HEAD_DIM)
    return y, (heads(kf), heads(vf), logf, s_fin, shift_hist[:, hist - 1], conv_hist[:, hist - (CONV_W - 1):])


def _sample_layer(x, pe, state_shift, state_wkv, state_conv, cache_k, cache_v, cache_logf, page_table, wts):
    nb, t_new, _ = x.shape
    n = nb * t_new
    tmajor = lambda a: a.transpose(1, 0, 2).reshape(1, n, a.shape[-1])
    bmajor = lambda a: a.reshape(t_new, nb, a.shape[-1]).transpose(1, 0, 2)
    x_tm = tmajor(x)
    (r, d, k, v, nkk, ka, g, bonus, q, kf, vf, logf, shift_hist) = _proj(
        x_tm, state_shift.reshape(1, nb, C_RW), wts, tm=nb, sh=nb, with_cumsum=False)
    to_scan = lambda a: a.reshape(t_new, nb, H_RW, HEAD_DIM).transpose(0, 2, 3, 1)
    wkv, s_fin = _scan([to_scan(a) for a in (r, d, k, nkk, ka, v)], state_wkv.transpose(1, 3, 2, 0), tt=t_new)
    wkv = wkv.transpose(0, 3, 1, 2).reshape(1, n, D_RW)
    s_fin = s_fin.transpose(3, 0, 2, 1)

    kf_b, vf_b, logf_b = bmajor(kf), bmajor(vf), bmajor(logf)
    head_mask = (jnp.arange(D_FX)[None, :] // HEAD_DIM == jnp.arange(H_FX)[:, None]).astype(BF16)
    qbig = (bmajor(q)[:, :, None, :] * head_mask).reshape(nb, t_new * H_FX, D_FX)
    nk = 2 * SUBLANES
    pad_rows = lambda a: jnp.pad(a, ((0, 0), (0, nk - t_new), (0, 0)))
    lf_new = jnp.pad(logf_b.transpose(0, 2, 1), ((0, 0), (0, 0), (0, LANES - t_new)))
    n_pool = cache_k.shape[0]
    fx = _dec_attn(page_table, qbig, pad_rows(kf_b), pad_rows(vf_b), lf_new,
                   cache_k.reshape(n_pool, PAGE_SIZE, D_FX), cache_v.reshape(n_pool, PAGE_SIZE, D_FX),
                   cache_logf.transpose(0, 2, 1), jnp.tile(head_mask.astype(F32), (t_new, 1)),
                   wts['fx_norm_g'], pc=8, t_new=t_new)
    x1 = _mix(x_tm, wkv, bonus, g, tmajor(fx), wts, tm=nb)
    conv0 = state_conv.transpose(1, 0, 2).reshape(1, (CONV_W - 1) * nb, D_FF)
    y, conv_hist = _ffn(x1, tmajor(pe), conv0, wts, tm=nb, sh=nb)
    heads = lambda a: a.reshape(nb, t_new, H_FX, HEAD_DIM)
    conv_new = conv_hist.reshape(CONV_W - 1, nb, D_FF).transpose(1, 0, 2)
    return bmajor(y[0]), (heads(kf_b), heads(vf_b), logf_b, s_fin, shift_hist[0], conv_new)


def kernel(x_prompt, x_sample, cache_k, cache_v, cache_logf, state_wkv, state_shift, state_conv, page_table, p_prompt, p_sample, ln_in_g, ln_in_b, w_in, b_f, rw_mu, rw_w0, rw_w2, rw_a0, rw_a2, rw_g2, rw_kk, rw_ka, rw_rk, rw_gn_g, rw_gn_b, fx_norm_g, w_o, ln1_g, ln1_b, w_up, conv_w, conv_b, w_down, ln2_g, ln2_b, w_ple, w_pg, b_pg):
    assert w_in.shape[0] == 1, "single-layer model"
    wts = _prep_weights(ln_in_g, ln_in_b, w_in[0], b_f[0], rw_mu[0], rw_w0[0], rw_w2[0], rw_a0[0], rw_a2[0],
                        rw_g2[0], rw_kk[0], rw_ka[0], rw_rk[0].reshape(-1), rw_gn_g[0], rw_gn_b[0], fx_norm_g[0],
                        w_o[0], ln1_g[0], ln1_b[0], w_up[0], conv_w[0], conv_b[0], w_down[0], ln2_g[0], ln2_b[0],
                        w_ple[0], w_pg[0], b_pg[0])
    yp, st_p = _prompt_layer(x_prompt, p_prompt[0], wts)
    ys, st_s = _sample_layer(x_sample, p_sample[0], state_shift[0], state_wkv[0], state_conv[0],
                             cache_k[0], cache_v[0], cache_logf[0], page_table, wts)
    return (yp, ys) + tuple(a[None] for a in st_p) + tuple(a[None] for a in st_s)
```

```python
import functools
import math

import jax
import jax.numpy as jnp
import numpy as np
from jax import lax
from jax.experimental import pallas as pl
from jax.experimental.pallas import tpu as pltpu

D_MODEL = 1024
HEAD_DIM = 64
D_RW = 512
H_RW = 8
D_FX = 512
H_FX = 8
R_DECAY = 64
R_ICL = 64
R_GATE = 128
C_RW = 3 * D_RW + R_DECAY + R_ICL + R_GATE
D_FF = 2816
CONV_W = 3
PLE_DIM = 256
PAGE_SIZE = 128
LN_EPS = 1e-5
GN_EPS = 64e-5
RMS_EPS = 1e-6
SCALE = HEAD_DIM ** -0.5
ALPHA = 2.0 ** 0.25
LOG2E = math.log2(math.e)

LANES = 128
SUBLANES = 8
PACKED_ROWS = 16
NEG_BIG = -1e30
VMEM_LIMIT = 56 * 1024 * 1024
EXTRA_ROWS = PACKED_ROWS
PIECES = 3

BF16 = jnp.bfloat16
F32 = jnp.float32


def _dot(a, b):
    return jnp.dot(a, b, preferred_element_type=F32)


def _dot_nt(a, b):
    return lax.dot_general(a, b, (((1,), (1,)), ((), ())), preferred_element_type=F32)


def _split3(x):
    hi = x.astype(BF16).astype(F32)
    r1 = x - hi
    mid = r1.astype(BF16).astype(F32)
    lo = (r1 - mid).astype(BF16).astype(F32)
    return hi, mid, lo


def _split2(x):
    hi = x.astype(BF16)
    lo = (x - hi.astype(F32)).astype(BF16)
    return hi, lo


def _segsum(x, bd):
    hi, lo = _split2(x)
    return _dot(hi, bd) + _dot(lo, bd)


def _rows_times(x, mat):
    n = x.shape[0]
    pieces = list(_split3(x))
    pad = (-PIECES * n) % PACKED_ROWS
    if pad:
        pieces.append(jnp.zeros((pad, x.shape[1]), F32))
    r = _dot(jnp.concatenate(pieces, axis=0).astype(BF16), mat)
    return r[0:n] + r[n:2 * n] + r[2 * n:3 * n]


def _layer_norm(x, g, b):
    mu = jnp.mean(x, -1, keepdims=True)
    xc = x - mu
    var = jnp.mean(xc * xc, -1, keepdims=True)
    return xc * lax.rsqrt(var + LN_EPS) * g + b


def _softplus(y):
    return jnp.maximum(y, 0.0) + jnp.log1p(jnp.exp(-jnp.abs(y)))


def _const_spec(shape):
    nd = len(shape)
    return pl.BlockSpec(shape, lambda *_: (0,) * nd, pipeline_mode=pl.Buffered(1))


def _params(sem):
    return pltpu.CompilerParams(dimension_semantics=sem, vmem_limit_bytes=VMEM_LIMIT)


_PROJ_COMMON = ['ln_in_g', 'ln_in_b', 'w_rw', 'rw_mu', 'rw_w0', 'rw_w2', 'rw_a0', 'rw_a2', 'rw_g2', 'rw_kk',
                'rw_ka', 'rw_rk', 'bd', 'w_q', 'w_f', 'b_f']
_PROJ_PROMPT = ['w_kt', 'w_vt', 'w_ft', 'b_ft', 'place_q', 'ones_q', 'place_k', 'ones_k']
_PROJ_SAMPLE = ['w_k', 'w_v']


def _proj_kernel(*refs, tm, sh, hist, prompt, q_scale):
    nc = len(_PROJ_COMMON)
    x_ref, hist0_ref = refs[:2]
    (lng, lnb, w_rw, mu, w0, w2, a0, a2, g2, kkw, kaw, rk, bd, w_q, w_f, b_f) = refs[2:2 + nc]
    pos = 2 + nc
    if prompt:
        (w_kt, w_vt, w_ft, b_ft, place_q, ones_q, place_k, ones_k, tril, triu) = refs[pos:pos + 10]
        pos += 10
    else:
        w_k, w_v = refs[pos:pos + 2]
        pos += 2
    (r_o, d_o, k_o, v_o, nkk_o, ka_o, g_o, bonus_o, q_o) = refs[pos:pos + 9]
    pos += 9
    if prompt:
        (kt_o, vt_o, logft_o, eq_o, ekt_o, shift_o, zext, carry_r, carry_t) = refs[pos:]
    else:
        (kf_o, vf_o, logf_o, shift_o, zext) = refs[pos:]

    t = pl.program_id(1)

    @pl.when(t == 0)
    def _():
        zext[0:hist, :] = hist0_ref[0]
        if prompt:
            carry_r[...] = jnp.zeros_like(carry_r)
            carry_t[...] = jnp.zeros_like(carry_t)

    xn = _layer_norm(x_ref[0], lng[...], lnb[...])
    xb = xn.astype(BF16)

    zr = _dot(xb, w_rw[...])
    zext[hist:hist + tm, :] = zr
    prev = zext[hist - sh:hist - sh + tm, :]
    zs = zr + (prev - zr) * mu[...]
    new_hist = zext[tm:tm + hist, :]
    shift_o[0] = new_hist
    zext[0:hist, :] = new_hist

    r = zs[:, 0:D_RW]
    k = zs[:, D_RW:2 * D_RW]
    v = zs[:, 2 * D_RW:3 * D_RW]
    lw = zs[:, 3 * D_RW:3 * D_RW + R_DECAY + R_ICL]
    xg = zs[:, 3 * D_RW + R_DECAY + R_ICL:C_RW]
    w_log = -_softplus(-(w0[...] + _dot(jnp.tanh(lw).astype(BF16), w2[...]))) - 0.5
    a = jax.nn.sigmoid(a0[...] + _dot(lw.astype(BF16), a2[...]))
    g = _dot(jax.nn.sigmoid(xg).astype(BF16), g2[...])
    kkr = k * kkw[...]
    kk = kkr / jnp.maximum(jnp.sqrt(_segsum(kkr * kkr, bd[...])), 1e-12)
    k_h = k * (1.0 + (a - 1.0) * kaw[...])
    r_o[0] = r
    d_o[0] = jnp.exp(-jnp.exp(w_log))
    k_o[0] = k_h
    v_o[0] = v
    nkk_o[0] = -kk
    ka_o[0] = kk * a
    g_o[0] = g
    bonus_o[0] = _segsum(r * k_h * rk[...], bd[...]) * v

    q_o[0] = (_dot(xb, w_q[...]) * q_scale).astype(BF16)
    logf = -_softplus(-(_dot(xb, w_f[...]) + b_f[...]))
    if not prompt:
        kf_o[0] = _dot(xb, w_k[...])
        vf_o[0] = _dot(xb, w_v[...])
        logf_o[0] = logf[:, 0:H_FX]
        return

    kt_o[0] = _dot_nt(w_kt[...], xb)
    vt_o[0] = _dot_nt(w_vt[...], xb)
    logft = -_softplus(-(_dot_nt(w_ft[...], xb)[0:H_FX] + b_ft[...]))
    logft_o[0] = logft
    cum_t = _rows_times(logft, triu[...]) + carry_t[...]
    carry_t[...] = cum_t[:, tm - 1:tm]
    pieces_t = list(_split3(cum_t * LOG2E)) + [jnp.zeros((SUBLANES, tm), F32)]
    ekt = _dot(place_k[...], jnp.concatenate(pieces_t, axis=0).astype(BF16)) + ones_k[...]
    ekt_o[0] = ekt.astype(BF16)

    hi, mid, lo = _split3(logf)
    trib = tril[...]
    cum_r = (_dot(trib, hi.astype(BF16)) + _dot(trib, mid.astype(BF16)) + _dot(trib, lo.astype(BF16))
             + carry_r[...])
    carry_r[...] = cum_r[tm - 1:tm, :]
    pieces_r = jnp.concatenate(_split3(cum_r * LOG2E), axis=1).astype(BF16)
    eq_o[0] = (_dot(pieces_r, place_q[...]) + ones_q[...]).astype(BF16)


def _bias_placement():
    pairs = H_FX // 2
    place_q = np.zeros((PIECES * LANES, pairs * LANES), np.float32)
    ones_q = np.zeros((1, pairs * LANES), np.float32)
    place_k = np.zeros((pairs * EXTRA_ROWS, (PIECES + 1) * SUBLANES), np.float32)
    ones_k = np.zeros((pairs * EXTRA_ROWS, 1), np.float32)
    for h in range(H_FX):
        p, o = h // 2, (h % 2) * 2 * PIECES
        for piece in range(PIECES):
            place_q[piece * LANES + h, p * LANES + o + piece] = 1.0
            ones_q[0, p * LANES + o + PIECES + piece] = 1.0
            ones_k[p * EXTRA_ROWS + o + piece, 0] = 1.0
            place_k[p * EXTRA_ROWS + o + PIECES + piece, piece * SUBLANES + h] = -1.0
    return (jnp.asarray(place_q, BF16), jnp.asarray(ones_q), jnp.asarray(place_k, BF16), jnp.asarray(ones_k))


def _proj(x, hist0, wts, *, tm, sh, prompt):
    nb, n, _ = x.shape
    hist = hist0.shape[1]
    consts = [wts[nm] for nm in _PROJ_COMMON + (_PROJ_PROMPT if prompt else _PROJ_SAMPLE)]
    if prompt:
        tril = jnp.tril(jnp.ones((tm, tm), BF16))
        consts += [tril, tril.T]
    row = lambda c, dt=F32: (jax.ShapeDtypeStruct((nb, n, c), dt), pl.BlockSpec((1, tm, c), lambda b, t: (b, t, 0)))
    col = lambda c, dt=F32: (jax.ShapeDtypeStruct((nb, c, n), dt), pl.BlockSpec((1, c, tm), lambda b, t: (b, 0, t)))
    outs = [row(D_RW) for _ in range(8)] + [row(D_FX, BF16)]
    scratch = [pltpu.VMEM((hist + tm, C_RW), F32)]
    if prompt:
        outs += [col(D_FX), col(D_FX), col(H_FX), row(D_FX, BF16), col(H_FX // 2 * EXTRA_ROWS, BF16)]
        scratch += [pltpu.VMEM((1, LANES), F32), pltpu.VMEM((H_FX, 1), F32)]
    else:
        outs += [row(D_FX), row(D_FX), row(H_FX)]
    outs.append((jax.ShapeDtypeStruct((nb, hist, C_RW), F32), pl.BlockSpec((1, hist, C_RW), lambda b, t: (b, 0, 0))))
    return pl.pallas_call(
        functools.partial(_proj_kernel, tm=tm, sh=sh, hist=hist, prompt=prompt,
                          q_scale=SCALE * LOG2E if prompt else SCALE),
        grid=(nb, n // tm),
        in_specs=[pl.BlockSpec((1, tm, D_MODEL), lambda b, t: (b, t, 0)),
                  pl.BlockSpec((1, hist, C_RW), lambda b, t: (b, 0, 0))]
                 + [_const_spec(c.shape) for c in consts],
        out_specs=[o[1] for o in outs],
        out_shape=[o[0] for o in outs],
        scratch_shapes=scratch,
        compiler_params=_params(("parallel", "arbitrary")),
        name="proj",
    )(x, hist0, *consts)


def _scan_kernel(r_ref, d_ref, k_ref, nkk_ref, ka_ref, v_ref, s0_ref, out_ref, s_ref, *, tt):
    @pl.when(pl.program_id(1) == 0)
    def _():
        s_ref[...] = s0_ref[...]

    def step(i, carry):
        row = lambda ref, j: ref[i, pl.ds(j, 1), :]
        v_t = v_ref[i]
        sa0 = jnp.zeros((HEAD_DIM, LANES), F32)
        sa1 = jnp.zeros((HEAD_DIM, LANES), F32)
        for j in range(0, HEAD_DIM, 2):
            sa0 = sa0 + s_ref[j] * row(nkk_ref, j)
            sa1 = sa1 + s_ref[j + 1] * row(nkk_ref, j + 1)
        sa = sa0 + sa1
        o0 = jnp.zeros((HEAD_DIM, LANES), F32)
        o1 = jnp.zeros((HEAD_DIM, LANES), F32)
        for j in range(HEAD_DIM):
            sj = s_ref[j] * row(d_ref, j) + sa * row(ka_ref, j) + v_t * row(k_ref, j)
            s_ref[j] = sj
            if j % 2 == 0:
                o0 = o0 + sj * row(r_ref, j)
            else:
                o1 = o1 + sj * row(r_ref, j)
        out_ref[i] = o0 + o1
        return carry

    lax.fori_loop(0, tt, step, 0)


def _scan(xs, s0, *, tt):
    steps, groups = xs[0].shape[:2]
    seq = pl.BlockSpec((tt, None, HEAD_DIM, LANES), lambda g, t: (t, g, 0, 0))
    st = pl.BlockSpec((None, HEAD_DIM, HEAD_DIM, LANES), lambda g, t: (g, 0, 0, 0))
    return pl.pallas_call(
        functools.partial(_scan_kernel, tt=tt),
        grid=(groups, steps // tt),
        in_specs=[seq] * 6 + [st],
        out_specs=[seq, st],
        out_shape=[jax.ShapeDtypeStruct(xs[0].shape, F32), jax.ShapeDtypeStruct(s0.shape, F32)],
        compiler_params=_params(("parallel", "arbitrary")),
        name="scan",
    )(*xs, s0)


def _attn_kernel(qi_ref, ki_ref, q_ref, eq_ref, kt_ref, vt_ref, ek_ref, g_ref, o_ref,
                 qa0, qa1, m0_s, m1_s, l0_s, l1_s, acc, *, tq, tk):
    step = pl.program_id(2)
    qi = qi_ref[step]
    ki = ki_ref[step]
    lane = lax.broadcasted_iota(jnp.int32, (1, LANES), 1)
    head0 = lane < HEAD_DIM

    @pl.when(ki == 0)
    def _():
        q2 = q_ref[0]
        eq = eq_ref[0]
        zero = jnp.zeros_like(q2)
        first = lane < 2 * PIECES
        qa0[...] = jnp.concatenate([jnp.where(head0, q2, zero), jnp.where(first, eq, zero)], axis=1)
        qa1[...] = jnp.concatenate([jnp.where(head0, zero, q2), jnp.where(first, zero, eq)], axis=1)
        m0_s[...] = jnp.full_like(m0_s, NEG_BIG)
        m1_s[...] = jnp.full_like(m1_s, NEG_BIG)
        l0_s[...] = jnp.zeros_like(l0_s)
        l1_s[...] = jnp.zeros_like(l1_s)
        acc[...] = jnp.zeros_like(acc)

    def update(masked):
        pad = jnp.zeros((LANES - EXTRA_ROWS, tk), BF16)
        kaug = jnp.concatenate([kt_ref[0].astype(BF16), ek_ref[0], pad], axis=0)
        vt2 = vt_ref[0]
        vz = jnp.zeros_like(vt2)
        sub = lax.broadcasted_iota(jnp.int32, (LANES, 1), 0)
        ones = jnp.where(lax.broadcasted_iota(jnp.int32, (EXTRA_ROWS, tk), 0) == 0, 1.0, 0.0).astype(BF16)
        if masked:
            causal = (lax.broadcasted_iota(jnp.int32, (tq, tk), 1)
                      <= lax.broadcasted_iota(jnp.int32, (tq, tk), 0))
        corr = []
        pv = []
        for qa, m_s, l_s, vh in ((qa0, m0_s, l0_s, jnp.where(sub < HEAD_DIM, vt2, vz)),
                                 (qa1, m1_s, l1_s, jnp.where(sub < HEAD_DIM, vz, vt2))):
            s = _dot(qa[...], kaug)
            if masked:
                s = jnp.where(causal, s, NEG_BIG)
            m_old = m_s[...]
            m_new = jnp.maximum(m_old, jnp.max(s, -1, keepdims=True))
            c = jnp.exp2(m_old - m_new)
            p = jnp.exp2(s - m_new).astype(BF16)
            vaug = jnp.concatenate([vh.astype(BF16), ones, pad], axis=0)
            out = _dot_nt(p, vaug)
            l_s[...] = l_s[...] * c + out[:, LANES:]
            m_s[...] = m_new
            corr.append(c)
            pv.append(out[:, :LANES])
        acc[...] = acc[...] * jnp.where(head0, corr[0], corr[1]) + pv[0] + pv[1]

    @pl.when(ki < qi)
    def _():
        update(False)

    @pl.when(ki == qi)
    def _():
        update(True)
        o = acc[...] * jnp.where(head0, 1.0 / l0_s[:, 0:1], 1.0 / l1_s[:, 0:1])
        o2 = o * o
        ms0 = jnp.sum(jnp.where(head0, o2, 0.0), -1, keepdims=True) * (1.0 / HEAD_DIM)
        ms1 = jnp.sum(jnp.where(head0, 0.0, o2), -1, keepdims=True) * (1.0 / HEAD_DIM)
        rn = jnp.where(head0, lax.rsqrt(ms0 + RMS_EPS), lax.rsqrt(ms1 + RMS_EPS))
        o_ref[0] = (o * rn * g_ref[...]).astype(BF16)


def _attn(q, eq, kt, vt, ekt, gain, *, tq):
    nb, s, _ = q.shape
    tk = tq
    nq = s // tq
    pairs = D_FX // LANES
    qi_tab = jnp.asarray([qi for qi in range(nq) for _ in range(qi + 1)], jnp.int32)
    ki_tab = jnp.asarray([ki for qi in range(nq) for ki in range(qi + 1)], jnp.int32)
    q_idx = lambda b, p, s_, qt, kt_: (b, qt[s_], p)
    k_idx = lambda b, p, s_, qt, kt_: (b, p, kt_[s_])
    return pl.pallas_call(
        functools.partial(_attn_kernel, tq=tq, tk=tk),
        grid_spec=pltpu.PrefetchScalarGridSpec(
            num_scalar_prefetch=2,
            grid=(nb, pairs, int(qi_tab.shape[0])),
            in_specs=[pl.BlockSpec((1, tq, LANES), q_idx),
                      pl.BlockSpec((1, tq, LANES), q_idx),
                      pl.BlockSpec((1, LANES, tk), k_idx),
                      pl.BlockSpec((1, LANES, tk), k_idx),
                      pl.BlockSpec((1, EXTRA_ROWS, tk), k_idx),
                      pl.BlockSpec((1, LANES), lambda b, p, s_, qt, kt_: (0, p))],
            out_specs=pl.BlockSpec((1, tq, LANES), q_idx),
            scratch_shapes=[pltpu.VMEM((tq, 2 * LANES), BF16), pltpu.VMEM((tq, 2 * LANES), BF16),
                            pltpu.VMEM((tq, 1), F32), pltpu.VMEM((tq, 1), F32),
                            pltpu.VMEM((tq, LANES), F32), pltpu.VMEM((tq, LANES), F32),
                            pltpu.VMEM((tq, LANES), F32)]),
        out_shape=jax.ShapeDtypeStruct((nb, s, D_FX), BF16),
        compiler_params=_params(("parallel", "parallel", "arbitrary")),
        name="attn",
    )(qi_tab, ki_tab, q, eq, kt, vt, ekt, gain)


def _dec_attn_kernel(pt_ref, q_ref, kn_ref, vn_ref, lfn_ref, gt_ref, hm_ref, g_ref, *rest, pc, nc, t_new):
    kp = rest[0:pc]
    vp = rest[pc:2 * pc]
    lp = rest[2 * pc:3 * pc]
    o_ref, m_s, l_s, acc, carry = rest[3 * pc:]
    c = pl.program_id(1)
    q = q_ref[0]
    per_query = lambda bias: jnp.concatenate([bias] * t_new, axis=0)

    def update(s, pv_of):
        m_old = m_s[...]
        m_new = jnp.maximum(m_old, jnp.max(s, -1, keepdims=True))
        corr = jnp.exp(m_old - m_new)
        p = jnp.exp(s - m_new)
        l_s[...] = l_s[...] * corr + jnp.sum(p, -1, keepdims=True)
        m_s[...] = m_new
        acc[...] = acc[...] * corr + pv_of(p.astype(BF16))

    @pl.when(c == 0)
    def _():
        m_s[...] = jnp.full_like(m_s, NEG_BIG)
        l_s[...] = jnp.zeros_like(l_s)
        acc[...] = jnp.zeros_like(acc)
        lf = lfn_ref[0]
        nk = kn_ref.shape[1]
        s = _dot_nt(q, kn_ref[0].astype(BF16)) + per_query(_rows_times(lf, gt_ref[...])[:, 0:nk])
        key_t = lax.broadcasted_iota(jnp.int32, s.shape, 1)
        qry_t = lax.broadcasted_iota(jnp.int32, s.shape, 0) // H_FX
        update(jnp.where(key_t <= qry_t, s, NEG_BIG), lambda p: _dot(p, vn_ref[0].astype(BF16)))
        carry[...] = jnp.sum(lf, -1, keepdims=True)

    lf_all = jnp.concatenate([lp[i][...] for i in range(pc)], axis=0)
    within = _rows_times(lf_all, gt_ref[...])
    totals = jnp.sum(lf_all, -1, keepdims=True)
    later = carry[...]
    scores = []
    for i in range(pc):
        rows = slice(i * H_FX, (i + 1) * H_FX)
        scores.append(_dot(q, kp[i][...].astype(BF16)) + per_query(within[rows] + later))
        later = later + totals[rows]
    carry[...] = later

    def pages_pv(p):
        out = _dot_nt(p[:, 0:PAGE_SIZE], vp[0][...].astype(BF16))
        for i in range(1, pc):
            out = out + _dot_nt(p[:, i * PAGE_SIZE:(i + 1) * PAGE_SIZE], vp[i][...].astype(BF16))
        return out

    update(jnp.concatenate(scores, axis=1), pages_pv)

    @pl.when(c == nc - 1)
    def _():
        o = acc[...] * (1.0 / l_s[...]) * hm_ref[...]
        ms = jnp.sum(o * o, -1, keepdims=True) * (1.0 / HEAD_DIM)
        o = o * lax.rsqrt(ms + RMS_EPS)
        per_t = [jnp.sum(o[t * H_FX:(t + 1) * H_FX], axis=0, keepdims=True) for t in range(t_new)]
        o_ref[0] = (jnp.concatenate(per_t, axis=0) * g_ref[...]).astype(BF16)


def _dec_attn(page_table, qbig, k_new, v_new, lf_new, cache_kt, cache_vt, cache_lft, head_mask, gain, *, pc, t_new):
    nb, n_pages = page_table.shape
    nc = n_pages // pc
    nk = k_new.shape[1]
    gt = jnp.triu(jnp.ones((PAGE_SIZE, PAGE_SIZE), BF16), 1).T
    rows = t_new * H_FX

    def page_spec(shape, i):
        return pl.BlockSpec((None,) + shape, lambda b, c, pt: (pt[b, n_pages - 1 - (c * pc + i)], 0, 0))

    per_b = lambda shape: pl.BlockSpec((1,) + shape, lambda b, c, pt: (b, 0, 0))
    const = lambda shape: pl.BlockSpec(shape, lambda b, c, pt: (0, 0))
    in_specs = ([per_b((rows, D_FX)), per_b((nk, D_FX)), per_b((nk, D_FX)), per_b((H_FX, LANES)),
                 const((PAGE_SIZE, PAGE_SIZE)), const((rows, D_FX)), const((1, D_FX))]
                + [page_spec((D_FX, PAGE_SIZE), i) for i in range(pc)]
                + [page_spec((D_FX, PAGE_SIZE), i) for i in range(pc)]
                + [page_spec((H_FX, PAGE_SIZE), i) for i in range(pc)])
    return pl.pallas_call(
        functools.partial(_dec_attn_kernel, pc=pc, nc=nc, t_new=t_new),
        grid_spec=pltpu.PrefetchScalarGridSpec(
            num_scalar_prefetch=1,
            grid=(nb, nc),
            in_specs=in_specs,
            out_specs=pl.BlockSpec((1, t_new, D_FX), lambda b, c, pt: (b, 0, 0)),
            scratch_shapes=[pltpu.VMEM((rows, 1), F32), pltpu.VMEM((rows, 1), F32),
                            pltpu.VMEM((rows, D_FX), F32), pltpu.VMEM((H_FX, 1), F32)]),
        out_shape=jax.ShapeDtypeStruct((nb, t_new, D_FX), BF16),
        compiler_params=_params(("parallel", "arbitrary")),
        name="dec_attn",
    )(page_table, qbig, k_new, v_new, lf_new, gt, head_mask, gain,
      *([cache_kt] * pc), *([cache_vt] * pc), *([cache_lft] * pc))


def _mix_kernel(x_ref, wkv_ref, bonus_ref, g_ref, fx_ref, lng, lnb, gng, gnb, bd, wo_rw, wo_fx, l1g, l1b, o_ref):
    xn = _layer_norm(x_ref[0], lng[...], lnb[...])
    wkv = wkv_ref[0]
    mean = _segsum(wkv, bd[...]) * (1.0 / HEAD_DIM)
    wc = wkv - mean
    var = _segsum(wc * wc, bd[...]) * (1.0 / HEAD_DIM)
    wn = wc * lax.rsqrt(var + GN_EPS) * gng[...] + gnb[...]
    rw_y = (wn + bonus_ref[0]) * g_ref[0]
    mix = _dot(rw_y.astype(BF16), wo_rw[...]) + _dot(fx_ref[0], wo_fx[...])
    o_ref[0] = _layer_norm(ALPHA * xn + mix, l1g[...], l1b[...])


def _mix(x, wkv, bonus, g, fx, wts, *, tm):
    nb, n, _ = x.shape
    names = ['ln_in_g', 'ln_in_b', 'rw_gn_g', 'rw_gn_b', 'bd', 'wo_rw', 'wo_fx', 'ln1_g', 'ln1_b']
    consts = [wts[nm] for nm in names]
    row = lambda c: pl.BlockSpec((1, tm, c), lambda b, t: (b, t, 0))
    return pl.pallas_call(
        _mix_kernel,
        grid=(nb, n // tm),
        in_specs=[row(D_MODEL), row(D_RW), row(D_RW), row(D_RW), row(D_FX)] + [_const_spec(c.shape) for c in consts],
        out_specs=row(D_MODEL),
        out_shape=jax.ShapeDtypeStruct((nb, n, D_MODEL), F32),
        compiler_params=_params(("parallel", "parallel")),
        name="mix",
    )(x, wkv, bonus, g, fx, *consts)


FF_CHUNK = D_FF // 2


def _ffn_kernel(x_ref, pe_ref, hist0_ref, wa, wb, cw, cb, wd, l2g, l2b, wpg, bpg, wple, y_ref, hist_o, aext,
                *, tm, sh, hist):
    @pl.when(pl.program_id(1) == 0)
    def _():
        aext[0:hist, :] = hist0_ref[0]

    x1 = x_ref[0]
    xb = x1.astype(BF16)
    y = jnp.zeros((tm, D_MODEL), F32)
    for c0 in range(0, D_FF, FF_CHUNK):
        cs = slice(c0, c0 + FF_CHUNK)
        a = _dot(xb, wa[:, cs])
        aext[hist:hist + tm, cs] = a
        p1 = aext[hist - sh:hist - sh + tm, cs]
        p2 = aext[hist - 2 * sh:hist - 2 * sh + tm, cs]
        conv = cb[:, cs] + p2 * cw[0:1, cs] + p1 * cw[1:2, cs] + a * cw[2:3, cs]
        gelu = 0.5 * conv * (1.0 + lax.erf(conv * (2.0 ** -0.5)))
        h = gelu * _dot(xb, wb[:, cs])
        y = y + _dot(h.astype(BF16), wd[cs, :])
    new_hist = aext[tm:tm + hist, :]
    hist_o[0] = new_hist
    aext[0:hist, :] = new_hist
    x2 = _layer_norm(ALPHA * x1 + y, l2g[...], l2b[...])
    gate = jax.nn.sigmoid(_dot(x2.astype(BF16), wpg[...]) + bpg[...])
    y_ref[0] = x2 + gate * _dot(pe_ref[0].astype(BF16), wple[...])


def _ffn(x1, pe, hist0, wts, *, tm, sh):
    nb, n, _ = x1.shape
    hist = hist0.shape[1]
    names = ['w_up_a', 'w_up_b', 'conv_w', 'conv_b', 'w_down', 'ln2_g', 'ln2_b', 'w_pg', 'b_pg', 'w_ple']
    consts = [wts[nm] for nm in names]
    row = lambda c: pl.BlockSpec((1, tm, c), lambda b, t: (b, t, 0))
    hist_spec = pl.BlockSpec((1, hist, D_FF), lambda b, t: (b, 0, 0))
    return pl.pallas_call(
        functools.partial(_ffn_kernel, tm=tm, sh=sh, hist=hist),
        grid=(nb, n // tm),
        in_specs=[row(D_MODEL), row(PLE_DIM), hist_spec] + [_const_spec(c.shape) for c in consts],
        out_specs=[row(D_MODEL), hist_spec],
        out_shape=[jax.ShapeDtypeStruct((nb, n, D_MODEL), F32), jax.ShapeDtypeStruct((nb, hist, D_FF), F32)],
        scratch_shapes=[pltpu.VMEM((hist + tm, D_FF), F32)],
        compiler_params=_params(("parallel", "arbitrary")),
        name="ffn",
    )(x1, pe, hist0, *consts)


def _prep_weights(ln_in_g, ln_in_b, w_in, b_f, rw_mu, rw_w0, rw_w2, rw_a0, rw_a2, rw_g2, rw_kk, rw_ka, rw_rk,
                  rw_gn_g, rw_gn_b, fx_norm_g, w_o, ln1_g, ln1_b, w_up, conv_w, conv_b, w_down, ln2_g, ln2_b,
                  w_ple, w_pg, b_pg):
    r2 = lambda a: a.reshape(1, -1)
    bf = lambda a: a.astype(BF16)
    lora_rows = R_DECAY + R_ICL
    seg = jnp.arange(D_RW) // HEAD_DIM
    fx0 = C_RW
    w_f = w_in[:, fx0 + 3 * D_FX:]
    place_q, ones_q, place_k, ones_k = _bias_placement()
    return dict(
        ln_in_g=r2(ln_in_g), ln_in_b=r2(ln_in_b),
        w_rw=bf(w_in[:, :C_RW]),
        w_q=bf(w_in[:, fx0:fx0 + D_FX]),
        w_k=bf(w_in[:, fx0 + D_FX:fx0 + 2 * D_FX]), w_v=bf(w_in[:, fx0 + 2 * D_FX:fx0 + 3 * D_FX]),
        w_kt=bf(w_in[:, fx0 + D_FX:fx0 + 2 * D_FX].T), w_vt=bf(w_in[:, fx0 + 2 * D_FX:fx0 + 3 * D_FX].T),
        w_f=bf(jnp.pad(w_f, ((0, 0), (0, LANES - H_FX)))),
        b_f=jnp.pad(b_f, (0, LANES - H_FX)).reshape(1, LANES),
        w_ft=bf(jnp.pad(w_f.T, ((0, PACKED_ROWS - H_FX), (0, 0)))), b_ft=b_f.reshape(H_FX, 1),
        place_q=place_q, ones_q=ones_q, place_k=place_k, ones_k=ones_k,
        rw_mu=r2(rw_mu), rw_w0=r2(rw_w0), rw_a0=r2(rw_a0),
        rw_w2=bf(jnp.pad(rw_w2, ((0, lora_rows - R_DECAY), (0, 0)))),
        rw_a2=bf(jnp.pad(rw_a2, ((lora_rows - R_ICL, 0), (0, 0)))),
        rw_g2=bf(rw_g2), rw_kk=r2(rw_kk), rw_ka=r2(rw_ka), rw_rk=r2(rw_rk),
        bd=(seg[:, None] == seg[None, :]).astype(BF16),
        rw_gn_g=r2(rw_gn_g), rw_gn_b=r2(rw_gn_b), fx_norm_g=r2(fx_norm_g),
        wo_rw=bf(w_o[:D_RW]), wo_fx=bf(w_o[D_RW:]), ln1_g=r2(ln1_g), ln1_b=r2(ln1_b),
        w_up_a=bf(w_up[:, :D_FF]), w_up_b=bf(w_up[:, D_FF:]), conv_w=conv_w, conv_b=r2(conv_b),
        w_down=bf(w_down), ln2_g=r2(ln2_g), ln2_b=r2(ln2_b), w_ple=bf(w_ple), w_pg=bf(w_pg), b_pg=r2(b_pg))


def _prompt_layer(x, pe, wts):
    nb, s, _ = x.shape
    hist = SUBLANES
    (r, d, k, v, nkk, ka, g, bonus, q, kt, vt, logft, eq, ekt, shift_hist) = _proj(
        x, jnp.zeros((nb, hist, C_RW), F32), wts, tm=256, sh=1, prompt=True)
    to_scan = lambda a: a.reshape(nb, s, H_RW, HEAD_DIM).transpose(1, 3, 0, 2).reshape(s, 1, HEAD_DIM, nb * H_RW)
    wkv, s_fin = _scan([to_scan(a) for a in (r, d, k, nkk, ka, v)],
                       jnp.zeros((1, HEAD_DIM, HEAD_DIM, nb * H_RW), F32), tt=16)
    wkv = wkv.reshape(s, HEAD_DIM, nb, H_RW).transpose(2, 0, 3, 1).reshape(nb, s, D_RW)
    s_fin = s_fin.reshape(HEAD_DIM, HEAD_DIM, nb, H_RW).transpose(2, 3, 1, 0)
    fx = _attn(q, eq, kt, vt, ekt, wts['fx_norm_g'], tq=512)
    x1 = _mix(x, wkv, bonus, g, fx, wts, tm=512)
    y, conv_hist = _ffn(x1, pe, jnp.zeros((nb, hist, D_FF), F32), wts, tm=256, sh=1)
    heads = lambda a: a.reshape(nb, H_FX, HEAD_DIM, s).transpose(0, 3, 1, 2)
    return y, (heads(kt), heads(vt), logft.transpose(0, 2, 1), s_fin, shift_hist[:, hist - 1],
               conv_hist[:, hist - (CONV_W - 1):])


def _sample_layer(x, pe, state_shift, state_wkv, state_conv, cache_k, cache_v, cache_logf, page_table, wts):
    nb, t_new, _ = x.shape
    n = nb * t_new
    tmajor = lambda a: a.transpose(1, 0, 2).reshape(1, n, a.shape[-1])
    bmajor = lambda a: a.reshape(t_new, nb, a.shape[-1]).transpose(1, 0, 2)
    x_tm = tmajor(x)
    (r, d, k, v, nkk, ka, g, bonus, q, kf, vf, logf, shift_hist) = _proj(
        x_tm, state_shift.reshape(1, nb, C_RW), wts, tm=nb, sh=nb, prompt=False)
    to_scan = lambda a: a.reshape(t_new, nb, H_RW, HEAD_DIM).transpose(0, 2, 3, 1)
    wkv, s_fin = _scan([to_scan(a) for a in (r, d, k, nkk, ka, v)], state_wkv.transpose(1, 3, 2, 0), tt=t_new)
    wkv = wkv.transpose(0, 3, 1, 2).reshape(1, n, D_RW)
    s_fin = s_fin.transpose(3, 0, 2, 1)

    kf_b, vf_b, logf_b = bmajor(kf), bmajor(vf), bmajor(logf)
    head_mask = (jnp.arange(D_FX)[None, :] // HEAD_DIM == jnp.arange(H_FX)[:, None]).astype(BF16)
    qbig = (bmajor(q)[:, :, None, :] * head_mask).reshape(nb, t_new * H_FX, D_FX)
    nk = PACKED_ROWS
    pad_rows = lambda a: jnp.pad(a, ((0, 0), (0, nk - t_new), (0, 0)))
    lf_new = jnp.pad(logf_b.transpose(0, 2, 1), ((0, 0), (0, 0), (0, LANES - t_new)))
    n_pool = cache_k.shape[0]
    pages_t = lambda c: c.transpose(0, 2, 3, 1).reshape(n_pool, D_FX, PAGE_SIZE)
    fx = _dec_attn(page_table, qbig, pad_rows(kf_b), pad_rows(vf_b), lf_new,
                   pages_t(cache_k), pages_t(cache_v), cache_logf.transpose(0, 2, 1),
                   jnp.tile(head_mask.astype(F32), (t_new, 1)), wts['fx_norm_g'], pc=16, t_new=t_new)
    x1 = _mix(x_tm, wkv, bonus, g, tmajor(fx), wts, tm=nb)
    conv0 = state_conv.transpose(1, 0, 2).reshape(1, (CONV_W - 1) * nb, D_FF)
    y, conv_hist = _ffn(x1, tmajor(pe), conv0, wts, tm=nb, sh=nb)
    heads = lambda a: a.reshape(nb, t_new, H_FX, HEAD_DIM)
    conv_new = conv_hist.reshape(CONV_W - 1, nb, D_FF).transpose(1, 0, 2)
    return bmajor(y[0]), (heads(kf_b), heads(vf_b), logf_b, s_fin, shift_hist[0], conv_new)


def kernel(x_prompt, x_sample, cache_k, cache_v, cache_logf, state_wkv, state_shift, state_conv, page_table, p_prompt, p_sample, ln_in_g, ln_in_b, w_in, b_f, rw_mu, rw_w0, rw_w2, rw_a0, rw_a2, rw_g2, rw_kk, rw_ka, rw_rk, rw_gn_g, rw_gn_b, fx_norm_g, w_o, ln1_g, ln1_b, w_up, conv_w, conv_b, w_down, ln2_g, ln2_b, w_ple, w_pg, b_pg):
    assert w_in.shape[0] == 1, "single-layer model"
    wts = _prep_weights(ln_in_g, ln_in_b, w_in[0], b_f[0], rw_mu[0], rw_w0[0], rw_w2[0], rw_a0[0], rw_a2[0],
                        rw_g2[0], rw_kk[0], rw_ka[0], rw_rk[0].reshape(-1), rw_gn_g[0], rw_gn_b[0], fx_norm_g[0],
                        w_o[0], ln1_g[0], ln1_b[0], w_up[0], conv_w[0], conv_b[0], w_down[0], ln2_g[0], ln2_b[0],
                        w_ple[0], w_pg[0], b_pg[0])
    yp, st_p = _prompt_layer(x_prompt, p_prompt[0], wts)
    ys, st_s = _sample_layer(x_sample, p_sample[0], state_shift[0], state_wkv[0], state_conv[0],
                             cache_k[0], cache_v[0], cache_logf[0], page_table, wts)
    return (yp, ys) + tuple(a[None] for a in st_p) + tuple(a[None] for a in st_s)
```

```python
import functools
import math

import jax
import jax.numpy as jnp
import numpy as np
from jax import lax
from jax.experimental import pallas as pl
from jax.experimental.pallas import tpu as pltpu

D_MODEL = 1024
HEAD_DIM = 64
D_RW = 512
H_RW = 8
D_FX = 512
H_FX = 8
R_DECAY = 64
R_ICL = 64
R_GATE = 128
C_RW = 3 * D_RW + R_DECAY + R_ICL + R_GATE
D_FF = 2816
CONV_W = 3
PLE_DIM = 256
PAGE_SIZE = 128
LN_EPS = 1e-5
GN_EPS = 64e-5
RMS_EPS = 1e-6
SCALE = HEAD_DIM ** -0.5
ALPHA = 2.0 ** 0.25
LOG2E = math.log2(math.e)

LANES = 128
SUBLANES = 8
PACKED_ROWS = 16
NEG_BIG = -1e30
VMEM_LIMIT = 56 * 1024 * 1024
EXTRA_ROWS = PACKED_ROWS
PIECES = 3

BF16 = jnp.bfloat16
F32 = jnp.float32


def _dot(a, b):
    return jnp.dot(a, b, preferred_element_type=F32)


def _dot_nt(a, b):
    return lax.dot_general(a, b, (((1,), (1,)), ((), ())), preferred_element_type=F32)


def _split3(x):
    hi = x.astype(BF16).astype(F32)
    r1 = x - hi
    mid = r1.astype(BF16).astype(F32)
    lo = (r1 - mid).astype(BF16).astype(F32)
    return hi, mid, lo


def _split2(x):
    hi = x.astype(BF16)
    lo = (x - hi.astype(F32)).astype(BF16)
    return hi, lo


def _segsum(x, bd):
    hi, lo = _split2(x)
    return _dot(hi, bd) + _dot(lo, bd)


def _rows_times(x, mat):
    n = x.shape[0]
    pieces = list(_split3(x))
    pad = (-PIECES * n) % PACKED_ROWS
    if pad:
        pieces.append(jnp.zeros((pad, x.shape[1]), F32))
    r = _dot(jnp.concatenate(pieces, axis=0).astype(BF16), mat)
    return r[0:n] + r[n:2 * n] + r[2 * n:3 * n]


def _layer_norm(x, g, b):
    mu = jnp.mean(x, -1, keepdims=True)
    xc = x - mu
    var = jnp.mean(xc * xc, -1, keepdims=True)
    return xc * lax.rsqrt(var + LN_EPS) * g + b


def _softplus(y):
    return jnp.maximum(y, 0.0) + jnp.log1p(jnp.exp(-jnp.abs(y)))


def _const_spec(shape):
    nd = len(shape)
    return pl.BlockSpec(shape, lambda *_: (0,) * nd, pipeline_mode=pl.Buffered(1))


def _params(sem):
    return pltpu.CompilerParams(dimension_semantics=sem, vmem_limit_bytes=VMEM_LIMIT)


_PROJ_COMMON = ['ln_in_g', 'ln_in_b', 'w_rw', 'rw_mu', 'rw_w0', 'rw_w2', 'rw_a0', 'rw_a2', 'rw_g2', 'rw_kk',
                'rw_ka', 'rw_rk', 'bd', 'w_k', 'w_f', 'b_f']
_PROJ_PROMPT = ['w_qt', 'w_kt', 'w_vt', 'w_ft', 'b_ft', 'place_q', 'ones_q', 'place_k', 'ones_k']
_PROJ_SAMPLE = ['w_q', 'w_v']


def _proj_kernel(*refs, tm, sh, hist, prompt, q_scale):
    nc = len(_PROJ_COMMON)
    x_ref, hist0_ref = refs[:2]
    (lng, lnb, w_rw, mu, w0, w2, a0, a2, g2, kkw, kaw, rk, bd, w_k, w_f, b_f) = refs[2:2 + nc]
    pos = 2 + nc
    if prompt:
        (w_qt, w_kt, w_vt, w_ft, b_ft, place_q, ones_q, place_k, ones_k, tril, triu) = refs[pos:pos + 11]
        pos += 11
    else:
        w_q, w_v = refs[pos:pos + 2]
        pos += 2
    (r_o, d_o, k_o, v_o, nkk_o, ka_o, g_o, bonus_o) = refs[pos:pos + 8]
    pos += 8
    if prompt:
        (qt_o, eqt_o, kr_o, ek_o, kt_o, vt_o, logft_o, shift_o, zext, carry_r, carry_t) = refs[pos:]
    else:
        (q_o, kf_o, vf_o, logf_o, shift_o, zext) = refs[pos:]

    t = pl.program_id(1)

    @pl.when(t == 0)
    def _():
        zext[0:hist, :] = hist0_ref[0]
        if prompt:
            carry_r[...] = jnp.zeros_like(carry_r)
            carry_t[...] = jnp.zeros_like(carry_t)

    xn = _layer_norm(x_ref[0], lng[...], lnb[...])
    xb = xn.astype(BF16)

    zr = _dot(xb, w_rw[...])
    zext[hist:hist + tm, :] = zr
    prev = zext[hist - sh:hist - sh + tm, :]
    zs = zr + (prev - zr) * mu[...]
    new_hist = zext[tm:tm + hist, :]
    shift_o[0] = new_hist
    zext[0:hist, :] = new_hist

    r = zs[:, 0:D_RW]
    k = zs[:, D_RW:2 * D_RW]
    v = zs[:, 2 * D_RW:3 * D_RW]
    lw = zs[:, 3 * D_RW:3 * D_RW + R_DECAY + R_ICL]
    xg = zs[:, 3 * D_RW + R_DECAY + R_ICL:C_RW]
    w_log = -_softplus(-(w0[...] + _dot(jnp.tanh(lw).astype(BF16), w2[...]))) - 0.5
    a = jax.nn.sigmoid(a0[...] + _dot(lw.astype(BF16), a2[...]))
    g = _dot(jax.nn.sigmoid(xg).astype(BF16), g2[...])
    kkr = k * kkw[...]
    kk = kkr / jnp.maximum(jnp.sqrt(_segsum(kkr * kkr, bd[...])), 1e-12)
    k_h = k * (1.0 + (a - 1.0) * kaw[...])
    r_o[0] = r
    d_o[0] = jnp.exp(-jnp.exp(w_log))
    k_o[0] = k_h
    v_o[0] = v
    nkk_o[0] = -kk
    ka_o[0] = kk * a
    g_o[0] = g
    bonus_o[0] = _segsum(r * k_h * rk[...], bd[...]) * v

    logf = -_softplus(-(_dot(xb, w_f[...]) + b_f[...]))
    kf = _dot(xb, w_k[...])
    if not prompt:
        q_o[0] = (_dot(xb, w_q[...]) * q_scale).astype(BF16)
        kf_o[0] = kf
        vf_o[0] = _dot(xb, w_v[...])
        logf_o[0] = logf[:, 0:H_FX]
        return

    qt_o[0] = (_dot_nt(w_qt[...], xb) * q_scale).astype(BF16)
    kr_o[0] = kf.astype(BF16)
    kt_o[0] = _dot_nt(w_kt[...], xb)
    vt_o[0] = _dot_nt(w_vt[...], xb)
    logft = -_softplus(-(_dot_nt(w_ft[...], xb)[0:H_FX] + b_ft[...]))
    logft_o[0] = logft
    cum_t = _rows_times(logft, triu[...]) + carry_t[...]
    carry_t[...] = cum_t[:, tm - 1:tm]
    pieces_t = list(_split3(cum_t * LOG2E)) + [jnp.zeros((SUBLANES, tm), F32)]
    eqt = _dot(place_q[...], jnp.concatenate(pieces_t, axis=0).astype(BF16)) + ones_q[...]
    eqt_o[0] = eqt.astype(BF16)

    hi, mid, lo = _split3(logf)
    trib = tril[...]
    cum_r = (_dot(trib, hi.astype(BF16)) + _dot(trib, mid.astype(BF16)) + _dot(trib, lo.astype(BF16))
             + carry_r[...])
    carry_r[...] = cum_r[tm - 1:tm, :]
    pieces_r = jnp.concatenate(_split3(cum_r * LOG2E), axis=1).astype(BF16)
    ek_o[0] = (_dot(pieces_r, place_k[...]) + ones_k[...]).astype(BF16)


def _bias_placement():
    pairs = H_FX // 2
    place_q = np.zeros((pairs * EXTRA_ROWS, (PIECES + 1) * SUBLANES), np.float32)
    ones_q = np.zeros((pairs * EXTRA_ROWS, 1), np.float32)
    place_k = np.zeros((PIECES * LANES, pairs * LANES), np.float32)
    ones_k = np.zeros((1, pairs * LANES), np.float32)
    for h in range(H_FX):
        p, o = h // 2, (h % 2) * 2 * PIECES
        for piece in range(PIECES):
            place_q[p * EXTRA_ROWS + o + piece, piece * SUBLANES + h] = 1.0
            ones_q[p * EXTRA_ROWS + o + PIECES + piece, 0] = 1.0
            ones_k[0, p * LANES + o + piece] = 1.0
            place_k[piece * LANES + h, p * LANES + o + PIECES + piece] = -1.0
    return (jnp.asarray(place_q, BF16), jnp.asarray(ones_q), jnp.asarray(place_k, BF16), jnp.asarray(ones_k))


def _proj(x, hist0, wts, *, tm, sh, prompt):
    nb, n, _ = x.shape
    hist = hist0.shape[1]
    consts = [wts[nm] for nm in _PROJ_COMMON + (_PROJ_PROMPT if prompt else _PROJ_SAMPLE)]
    if prompt:
        tril = jnp.tril(jnp.ones((tm, tm), BF16))
        consts += [tril, tril.T]
    row = lambda c, dt=F32: (jax.ShapeDtypeStruct((nb, n, c), dt), pl.BlockSpec((1, tm, c), lambda b, t: (b, t, 0)))
    col = lambda c, dt=F32: (jax.ShapeDtypeStruct((nb, c, n), dt), pl.BlockSpec((1, c, tm), lambda b, t: (b, 0, t)))
    outs = [row(D_RW) for _ in range(8)]
    scratch = [pltpu.VMEM((hist + tm, C_RW), F32)]
    if prompt:
        outs += [col(D_FX, BF16), col(H_FX // 2 * EXTRA_ROWS, BF16), row(D_FX, BF16), row(D_FX, BF16),
                 col(D_FX), col(D_FX), col(H_FX)]
        scratch += [pltpu.VMEM((1, LANES), F32), pltpu.VMEM((H_FX, 1), F32)]
    else:
        outs += [row(D_FX, BF16), row(D_FX), row(D_FX), row(H_FX)]
    outs.append((jax.ShapeDtypeStruct((nb, hist, C_RW), F32), pl.BlockSpec((1, hist, C_RW), lambda b, t: (b, 0, 0))))
    return pl.pallas_call(
        functools.partial(_proj_kernel, tm=tm, sh=sh, hist=hist, prompt=prompt,
                          q_scale=SCALE * LOG2E if prompt else SCALE),
        grid=(nb, n // tm),
        in_specs=[pl.BlockSpec((1, tm, D_MODEL), lambda b, t: (b, t, 0)),
                  pl.BlockSpec((1, hist, C_RW), lambda b, t: (b, 0, 0))]
                 + [_const_spec(c.shape) for c in consts],
        out_specs=[o[1] for o in outs],
        out_shape=[o[0] for o in outs],
        scratch_shapes=scratch,
        compiler_params=_params(("parallel", "arbitrary")),
        name="proj",
    )(x, hist0, *consts)


def _scan_kernel(r_ref, d_ref, k_ref, nkk_ref, ka_ref, v_ref, s0_ref, out_ref, s_ref, *, tt):
    @pl.when(pl.program_id(1) == 0)
    def _():
        s_ref[...] = s0_ref[...]

    half = HEAD_DIM // 2

    def step(i, carry):
        row = lambda ref, j: ref[i, pl.ds(j, 1), :]
        for i0 in (0, half):
            rows = pl.ds(i0, half)
            sa0 = jnp.zeros((half, LANES), F32)
            sa1 = jnp.zeros((half, LANES), F32)
            for j in range(0, HEAD_DIM, 2):
                sa0 = sa0 + s_ref[j, rows, :] * row(nkk_ref, j)
                sa1 = sa1 + s_ref[j + 1, rows, :] * row(nkk_ref, j + 1)
            sa = sa0 + sa1
            v_t = v_ref[i, rows, :]
            o0 = jnp.zeros((half, LANES), F32)
            o1 = jnp.zeros((half, LANES), F32)
            for j in range(HEAD_DIM):
                sj = s_ref[j, rows, :] * row(d_ref, j) + sa * row(ka_ref, j) + v_t * row(k_ref, j)
                s_ref[j, rows, :] = sj
                if j % 2 == 0:
                    o0 = o0 + sj * row(r_ref, j)
                else:
                    o1 = o1 + sj * row(r_ref, j)
            out_ref[i, rows, :] = o0 + o1
        return carry

    lax.fori_loop(0, tt, step, 0)


def _scan(xs, s0, *, tt):
    steps, groups = xs[0].shape[:2]
    seq = pl.BlockSpec((tt, None, HEAD_DIM, LANES), lambda g, t: (t, g, 0, 0))
    st = pl.BlockSpec((None, HEAD_DIM, HEAD_DIM, LANES), lambda g, t: (g, 0, 0, 0))
    return pl.pallas_call(
        functools.partial(_scan_kernel, tt=tt),
        grid=(groups, steps // tt),
        in_specs=[seq] * 6 + [st],
        out_specs=[seq, st],
        out_shape=[jax.ShapeDtypeStruct(xs[0].shape, F32), jax.ShapeDtypeStruct(s0.shape, F32)],
        compiler_params=_params(("parallel", "arbitrary")),
        name="scan",
    )(*xs, s0)


def _attn_kernel(qi_ref, ki_ref, qt_ref, eqt_ref, kr_ref, ek_ref, vt_ref, g_ref, o_ref,
                 qa0, qa1, m0_s, m1_s, l0_s, l1_s, acc, *, tq, tk):
    step = pl.program_id(2)
    qi = qi_ref[step]
    ki = ki_ref[step]
    sub = lax.broadcasted_iota(jnp.int32, (LANES, 1), 0)
    head0 = sub < HEAD_DIM

    @pl.when(ki == 0)
    def _():
        q2 = qt_ref[0]
        eq = eqt_ref[0]
        zero = jnp.zeros_like(q2)
        ezero = jnp.zeros_like(eq)
        first = lax.broadcasted_iota(jnp.int32, (EXTRA_ROWS, 1), 0) < 2 * PIECES
        pad = jnp.zeros((LANES - EXTRA_ROWS, tq), BF16)
        qa0[...] = jnp.concatenate([jnp.where(head0, q2, zero), jnp.where(first, eq, ezero), pad], axis=0)
        qa1[...] = jnp.concatenate([jnp.where(head0, zero, q2), jnp.where(first, ezero, eq), pad], axis=0)
        m0_s[...] = jnp.full_like(m0_s, NEG_BIG)
        m1_s[...] = jnp.full_like(m1_s, NEG_BIG)
        l0_s[...] = jnp.zeros_like(l0_s)
        l1_s[...] = jnp.zeros_like(l1_s)
        acc[...] = jnp.zeros_like(acc)

    def update(masked):
        kaug = jnp.concatenate([kr_ref[0], ek_ref[0]], axis=1)
        vt2 = vt_ref[0]
        vz = jnp.zeros_like(vt2)
        pad = jnp.zeros((LANES - EXTRA_ROWS, tk), BF16)
        ones = jnp.where(lax.broadcasted_iota(jnp.int32, (EXTRA_ROWS, tk), 0) == 0, 1.0, 0.0).astype(BF16)
        vaug = [jnp.concatenate([jnp.where(head0, vt2, vz).astype(BF16), ones, pad], axis=0),
                jnp.concatenate([jnp.where(head0, vz, vt2).astype(BF16), ones, pad], axis=0)]
        scores = [_dot(kaug, qa0[...]), _dot(kaug, qa1[...])]
        if masked:
            causal = (lax.broadcasted_iota(jnp.int32, (tk, tq), 0)
                      <= lax.broadcasted_iota(jnp.int32, (tk, tq), 1))
            scores = [jnp.where(causal, s, NEG_BIG) for s in scores]
        corr = []
        outs = []
        for s, m_s in zip(scores, (m0_s, m1_s)):
            m_old = m_s[...]
            m_new = jnp.maximum(m_old, jnp.max(s, axis=0, keepdims=True))
            m_s[...] = m_new
            corr.append(jnp.exp2(m_old - m_new))
            outs.append(jnp.exp2(s - m_new).astype(BF16))
        outs = [_dot(va, p) for va, p in zip(vaug, outs)]
        l0_s[...] = l0_s[...] * corr[0] + outs[0][LANES:LANES + SUBLANES]
        l1_s[...] = l1_s[...] * corr[1] + outs[1][LANES:LANES + SUBLANES]
        acc[...] = acc[...] * jnp.where(head0, corr[0], corr[1]) + outs[0][:LANES] + outs[1][:LANES]

    @pl.when(ki < qi)
    def _():
        update(False)

    @pl.when(ki == qi)
    def _():
        update(True)
        ot = acc[...] * jnp.where(head0, 1.0 / l0_s[0:1, :], 1.0 / l1_s[0:1, :])
        o = ot.T
        o2 = o * o
        lane0 = lax.broadcasted_iota(jnp.int32, (1, LANES), 1) < HEAD_DIM
        ms0 = jnp.sum(jnp.where(lane0, o2, 0.0), -1, keepdims=True) * (1.0 / HEAD_DIM)
        ms1 = jnp.sum(jnp.where(lane0, 0.0, o2), -1, keepdims=True) * (1.0 / HEAD_DIM)
        rn = jnp.where(lane0, lax.rsqrt(ms0 + RMS_EPS), lax.rsqrt(ms1 + RMS_EPS))
        o_ref[0] = (o * rn * g_ref[...]).astype(BF16)


def _attn(qt, eqt, kr, ek, vt, gain, *, tq):
    nb, _, s = qt.shape
    tk = tq
    nq = s // tq
    pairs = D_FX // LANES
    qi_tab = jnp.asarray([qi for qi in range(nq) for _ in range(qi + 1)], jnp.int32)
    ki_tab = jnp.asarray([ki for qi in range(nq) for ki in range(qi + 1)], jnp.int32)
    q_col = lambda b, p, s_, qt_, kt_: (b, p, qt_[s_])
    k_row = lambda b, p, s_, qt_, kt_: (b, kt_[s_], p)
    return pl.pallas_call(
        functools.partial(_attn_kernel, tq=tq, tk=tk),
        grid_spec=pltpu.PrefetchScalarGridSpec(
            num_scalar_prefetch=2,
            grid=(nb, pairs, int(qi_tab.shape[0])),
            in_specs=[pl.BlockSpec((1, LANES, tq), q_col),
                      pl.BlockSpec((1, EXTRA_ROWS, tq), q_col),
                      pl.BlockSpec((1, tk, LANES), k_row),
                      pl.BlockSpec((1, tk, LANES), k_row),
                      pl.BlockSpec((1, LANES, tk), lambda b, p, s_, qt_, kt_: (b, p, kt_[s_])),
                      pl.BlockSpec((1, LANES), lambda b, p, s_, qt_, kt_: (0, p))],
            out_specs=pl.BlockSpec((1, tq, LANES), lambda b, p, s_, qt_, kt_: (b, qt_[s_], p)),
            scratch_shapes=[pltpu.VMEM((2 * LANES, tq), BF16), pltpu.VMEM((2 * LANES, tq), BF16),
                            pltpu.VMEM((1, tq), F32), pltpu.VMEM((1, tq), F32),
                            pltpu.VMEM((SUBLANES, tq), F32), pltpu.VMEM((SUBLANES, tq), F32),
                            pltpu.VMEM((LANES, tq), F32)]),
        out_shape=jax.ShapeDtypeStruct((nb, s, D_FX), BF16),
        compiler_params=_params(("parallel", "parallel", "arbitrary")),
        name="attn",
    )(qi_tab, ki_tab, qt, eqt, kr, ek, vt, gain)


def _dec_attn_kernel(pt_ref, q_ref, kn_ref, vn_ref, lfn_ref, gt_ref, hm_ref, g_ref, *rest, pc, nc, t_new):
    kp = rest[0:pc]
    vp = rest[pc:2 * pc]
    lp = rest[2 * pc:3 * pc]
    o_ref, m_s, l_s, acc, carry = rest[3 * pc:]
    c = pl.program_id(1)
    q = q_ref[0]
    per_query = lambda bias: jnp.concatenate([bias] * t_new, axis=0)

    def update(s, pv_of):
        m_old = m_s[...]
        m_new = jnp.maximum(m_old, jnp.max(s, -1, keepdims=True))
        corr = jnp.exp(m_old - m_new)
        p = jnp.exp(s - m_new)
        l_s[...] = l_s[...] * corr + jnp.sum(p, -1, keepdims=True)
        m_s[...] = m_new
        acc[...] = acc[...] * corr + pv_of(p.astype(BF16))

    @pl.when(c == 0)
    def _():
        m_s[...] = jnp.full_like(m_s, NEG_BIG)
        l_s[...] = jnp.zeros_like(l_s)
        acc[...] = jnp.zeros_like(acc)
        lf = lfn_ref[0]
        nk = kn_ref.shape[1]
        s = _dot_nt(q, kn_ref[0].astype(BF16)) + per_query(_rows_times(lf, gt_ref[...])[:, 0:nk])
        key_t = lax.broadcasted_iota(jnp.int32, s.shape, 1)
        qry_t = lax.broadcasted_iota(jnp.int32, s.shape, 0) // H_FX
        update(jnp.where(key_t <= qry_t, s, NEG_BIG), lambda p: _dot(p, vn_ref[0].astype(BF16)))
        carry[...] = jnp.sum(lf, -1, keepdims=True)

    lf_all = jnp.concatenate([lp[i][...] for i in range(pc)], axis=0)
    within = _rows_times(lf_all, gt_ref[...])
    totals = jnp.sum(lf_all, -1, keepdims=True)
    later = carry[...]
    scores = []
    for i in range(pc):
        rows = slice(i * H_FX, (i + 1) * H_FX)
        scores.append(_dot(q, kp[i][...].astype(BF16)) + per_query(within[rows] + later))
        later = later + totals[rows]
    carry[...] = later

    def pages_pv(p):
        out = _dot_nt(p[:, 0:PAGE_SIZE], vp[0][...].astype(BF16))
        for i in range(1, pc):
            out = out + _dot_nt(p[:, i * PAGE_SIZE:(i + 1) * PAGE_SIZE], vp[i][...].astype(BF16))
        return out

    update(jnp.concatenate(scores, axis=1), pages_pv)

    @pl.when(c == nc - 1)
    def _():
        o = acc[...] * (1.0 / l_s[...]) * hm_ref[...]
        ms = jnp.sum(o * o, -1, keepdims=True) * (1.0 / HEAD_DIM)
        o = o * lax.rsqrt(ms + RMS_EPS)
        per_t = [jnp.sum(o[t * H_FX:(t + 1) * H_FX], axis=0, keepdims=True) for t in range(t_new)]
        o_ref[0] = (jnp.concatenate(per_t, axis=0) * g_ref[...]).astype(BF16)


def _dec_attn(page_table, qbig, k_new, v_new, lf_new, cache_kt, cache_vt, cache_lft, head_mask, gain, *, pc, t_new):
    nb, n_pages = page_table.shape
    nc = n_pages // pc
    nk = k_new.shape[1]
    gt = jnp.triu(jnp.ones((PAGE_SIZE, PAGE_SIZE), BF16), 1).T
    rows = t_new * H_FX

    def page_spec(shape, i):
        return pl.BlockSpec((None,) + shape, lambda b, c, pt: (pt[b, n_pages - 1 - (c * pc + i)], 0, 0))

    per_b = lambda shape: pl.BlockSpec((1,) + shape, lambda b, c, pt: (b, 0, 0))
    const = lambda shape: pl.BlockSpec(shape, lambda b, c, pt: (0, 0))
    in_specs = ([per_b((rows, D_FX)), per_b((nk, D_FX)), per_b((nk, D_FX)), per_b((H_FX, LANES)),
                 const((PAGE_SIZE, PAGE_SIZE)), const((rows, D_FX)), const((1, D_FX))]
                + [page_spec((D_FX, PAGE_SIZE), i) for i in range(pc)]
                + [page_spec((D_FX, PAGE_SIZE), i) for i in range(pc)]
                + [page_spec((H_FX, PAGE_SIZE), i) for i in range(pc)])
    return pl.pallas_call(
        functools.partial(_dec_attn_kernel, pc=pc, nc=nc, t_new=t_new),
        grid_spec=pltpu.PrefetchScalarGridSpec(
            num_scalar_prefetch=1,
            grid=(nb, nc),
            in_specs=in_specs,
            out_specs=pl.BlockSpec((1, t_new, D_FX), lambda b, c, pt: (b, 0, 0)),
            scratch_shapes=[pltpu.VMEM((rows, 1), F32), pltpu.VMEM((rows, 1), F32),
                            pltpu.VMEM((rows, D_FX), F32), pltpu.VMEM((H_FX, 1), F32)]),
        out_shape=jax.ShapeDtypeStruct((nb, t_new, D_FX), BF16),
        compiler_params=_params(("parallel", "arbitrary")),
        name="dec_attn",
    )(page_table, qbig, k_new, v_new, lf_new, gt, head_mask, gain,
      *([cache_kt] * pc), *([cache_vt] * pc), *([cache_lft] * pc))


def _mix_kernel(x_ref, wkv_ref, bonus_ref, g_ref, fx_ref, lng, lnb, gng, gnb, bd, wo_rw, wo_fx, l1g, l1b, o_ref):
    xn = _layer_norm(x_ref[0], lng[...], lnb[...])
    wkv = wkv_ref[0]
    mean = _segsum(wkv, bd[...]) * (1.0 / HEAD_DIM)
    wc = wkv - mean
    var = _segsum(wc * wc, bd[...]) * (1.0 / HEAD_DIM)
    wn = wc * lax.rsqrt(var + GN_EPS) * gng[...] + gnb[...]
    rw_y = (wn + bonus_ref[0]) * g_ref[0]
    mix = _dot(rw_y.astype(BF16), wo_rw[...]) + _dot(fx_ref[0], wo_fx[...])
    o_ref[0] = _layer_norm(ALPHA * xn + mix, l1g[...], l1b[...])


def _mix(x, wkv, bonus, g, fx, wts, *, tm):
    nb, n, _ = x.shape
    names = ['ln_in_g', 'ln_in_b', 'rw_gn_g', 'rw_gn_b', 'bd', 'wo_rw', 'wo_fx', 'ln1_g', 'ln1_b']
    consts = [wts[nm] for nm in names]
    row = lambda c: pl.BlockSpec((1, tm, c), lambda b, t: (b, t, 0))
    return pl.pallas_call(
        _mix_kernel,
        grid=(nb, n // tm),
        in_specs=[row(D_MODEL), row(D_RW), row(D_RW), row(D_RW), row(D_FX)] + [_const_spec(c.shape) for c in consts],
        out_specs=row(D_MODEL),
        out_shape=jax.ShapeDtypeStruct((nb, n, D_MODEL), F32),
        compiler_params=_params(("parallel", "parallel")),
        name="mix",
    )(x, wkv, bonus, g, fx, *consts)


FF_CHUNK = D_FF // 2


def _ffn_kernel(x_ref, pe_ref, hist0_ref, wa, wb, cw, cb, wd, l2g, l2b, wpg, bpg, wple, y_ref, hist_o, aext,
                *, tm, sh, hist):
    @pl.when(pl.program_id(1) == 0)
    def _():
        aext[0:hist, :] = hist0_ref[0]

    x1 = x_ref[0]
    xb = x1.astype(BF16)
    y = jnp.zeros((tm, D_MODEL), F32)
    for c0 in range(0, D_FF, FF_CHUNK):
        cs = slice(c0, c0 + FF_CHUNK)
        a = _dot(xb, wa[:, cs])
        aext[hist:hist + tm, cs] = a
        p1 = aext[hist - sh:hist - sh + tm, cs]
        p2 = aext[hist - 2 * sh:hist - 2 * sh + tm, cs]
        conv = cb[:, cs] + p2 * cw[0:1, cs] + p1 * cw[1:2, cs] + a * cw[2:3, cs]
        gelu = 0.5 * conv * (1.0 + lax.erf(conv * (2.0 ** -0.5)))
        h = gelu * _dot(xb, wb[:, cs])
        y = y + _dot(h.astype(BF16), wd[cs, :])
    new_hist = aext[tm:tm + hist, :]
    hist_o[0] = new_hist
    aext[0:hist, :] = new_hist
    x2 = _layer_norm(ALPHA * x1 + y, l2g[...], l2b[...])
    gate = jax.nn.sigmoid(_dot(x2.astype(BF16), wpg[...]) + bpg[...])
    y_ref[0] = x2 + gate * _dot(pe_ref[0].astype(BF16), wple[...])


def _ffn(x1, pe, hist0, wts, *, tm, sh):
    nb, n, _ = x1.shape
    hist = hist0.shape[1]
    names = ['w_up_a', 'w_up_b', 'conv_w', 'conv_b', 'w_down', 'ln2_g', 'ln2_b', 'w_pg', 'b_pg', 'w_ple']
    consts = [wts[nm] for nm in names]
    row = lambda c: pl.BlockSpec((1, tm, c), lambda b, t: (b, t, 0))
    hist_spec = pl.BlockSpec((1, hist, D_FF), lambda b, t: (b, 0, 0))
    return pl.pallas_call(
        functools.partial(_ffn_kernel, tm=tm, sh=sh, hist=hist),
        grid=(nb, n // tm),
        in_specs=[row(D_MODEL), row(PLE_DIM), hist_spec] + [_const_spec(c.shape) for c in consts],
        out_specs=[row(D_MODEL), hist_spec],
        out_shape=[jax.ShapeDtypeStruct((nb, n, D_MODEL), F32), jax.ShapeDtypeStruct((nb, hist, D_FF), F32)],
        scratch_shapes=[pltpu.VMEM((hist + tm, D_FF), F32)],
        compiler_params=_params(("parallel", "arbitrary")),
        name="ffn",
    )(x1, pe, hist0, *consts)


def _prep_weights(ln_in_g, ln_in_b, w_in, b_f, rw_mu, rw_w0, rw_w2, rw_a0, rw_a2, rw_g2, rw_kk, rw_ka, rw_rk,
                  rw_gn_g, rw_gn_b, fx_norm_g, w_o, ln1_g, ln1_b, w_up, conv_w, conv_b, w_down, ln2_g, ln2_b,
                  w_ple, w_pg, b_pg):
    r2 = lambda a: a.reshape(1, -1)
    bf = lambda a: a.astype(BF16)
    lora_rows = R_DECAY + R_ICL
    seg = jnp.arange(D_RW) // HEAD_DIM
    fx0 = C_RW
    w_f = w_in[:, fx0 + 3 * D_FX:]
    place_q, ones_q, place_k, ones_k = _bias_placement()
    return dict(
        ln_in_g=r2(ln_in_g), ln_in_b=r2(ln_in_b),
        w_rw=bf(w_in[:, :C_RW]),
        w_q=bf(w_in[:, fx0:fx0 + D_FX]),
        w_k=bf(w_in[:, fx0 + D_FX:fx0 + 2 * D_FX]), w_v=bf(w_in[:, fx0 + 2 * D_FX:fx0 + 3 * D_FX]),
        w_qt=bf(w_in[:, fx0:fx0 + D_FX].T), w_kt=bf(w_in[:, fx0 + D_FX:fx0 + 2 * D_FX].T), w_vt=bf(w_in[:, fx0 + 2 * D_FX:fx0 + 3 * D_FX].T),
        w_f=bf(jnp.pad(w_f, ((0, 0), (0, LANES - H_FX)))),
        b_f=jnp.pad(b_f, (0, LANES - H_FX)).reshape(1, LANES),
        w_ft=bf(jnp.pad(w_f.T, ((0, PACKED_ROWS - H_FX), (0, 0)))), b_ft=b_f.reshape(H_FX, 1),
        place_q=place_q, ones_q=ones_q, place_k=place_k, ones_k=ones_k,
        rw_mu=r2(rw_mu), rw_w0=r2(rw_w0), rw_a0=r2(rw_a0),
        rw_w2=bf(jnp.pad(rw_w2, ((0, lora_rows - R_DECAY), (0, 0)))),
        rw_a2=bf(jnp.pad(rw_a2, ((lora_rows - R_ICL, 0), (0, 0)))),
        rw_g2=bf(rw_g2), rw_kk=r2(rw_kk), rw_ka=r2(rw_ka), rw_rk=r2(rw_rk),
        bd=(seg[:, None] == seg[None, :]).astype(BF16),
        rw_gn_g=r2(rw_gn_g), rw_gn_b=r2(rw_gn_b), fx_norm_g=r2(fx_norm_g),
        wo_rw=bf(w_o[:D_RW]), wo_fx=bf(w_o[D_RW:]), ln1_g=r2(ln1_g), ln1_b=r2(ln1_b),
        w_up_a=bf(w_up[:, :D_FF]), w_up_b=bf(w_up[:, D_FF:]), conv_w=conv_w, conv_b=r2(conv_b),
        w_down=bf(w_down), ln2_g=r2(ln2_g), ln2_b=r2(ln2_b), w_ple=bf(w_ple), w_pg=bf(w_pg), b_pg=r2(b_pg))


def _prompt_layer(x, pe, wts):
    nb, s, _ = x.shape
    hist = SUBLANES
    (r, d, k, v, nkk, ka, g, bonus, qt, eqt, kr, ek, kt, vt, logft, shift_hist) = _proj(
        x, jnp.zeros((nb, hist, C_RW), F32), wts, tm=256, sh=1, prompt=True)
    to_scan = lambda a: a.reshape(nb, s, H_RW, HEAD_DIM).transpose(1, 3, 0, 2).reshape(s, 1, HEAD_DIM, nb * H_RW)
    wkv, s_fin = _scan([to_scan(a) for a in (r, d, k, nkk, ka, v)],
                       jnp.zeros((1, HEAD_DIM, HEAD_DIM, nb * H_RW), F32), tt=16)
    wkv = wkv.reshape(s, HEAD_DIM, nb, H_RW).transpose(2, 0, 3, 1).reshape(nb, s, D_RW)
    s_fin = s_fin.reshape(HEAD_DIM, HEAD_DIM, nb, H_RW).transpose(2, 3, 1, 0)
    fx = _attn(qt, eqt, kr, ek, vt, wts['fx_norm_g'], tq=512)
    x1 = _mix(x, wkv, bonus, g, fx, wts, tm=512)
    y, conv_hist = _ffn(x1, pe, jnp.zeros((nb, hist, D_FF), F32), wts, tm=256, sh=1)
    heads = lambda a: a.reshape(nb, H_FX, HEAD_DIM, s).transpose(0, 3, 1, 2)
    return y, (heads(kt), heads(vt), logft.transpose(0, 2, 1), s_fin, shift_hist[:, hist - 1],
               conv_hist[:, hist - (CONV_W - 1):])


def _sample_layer(x, pe, state_shift, state_wkv, state_conv, cache_k, cache_v, cache_logf, page_table, wts):
    nb, t_new, _ = x.shape
    n = nb * t_new
    tmajor = lambda a: a.transpose(1, 0, 2).reshape(1, n, a.shape[-1])
    bmajor = lambda a: a.reshape(t_new, nb, a.shape[-1]).transpose(1, 0, 2)
    x_tm = tmajor(x)
    (r, d, k, v, nkk, ka, g, bonus, q, kf, vf, logf, shift_hist) = _proj(
        x_tm, state_shift.reshape(1, nb, C_RW), wts, tm=nb, sh=nb, prompt=False)
    to_scan = lambda a: a.reshape(t_new, nb, H_RW, HEAD_DIM).transpose(0, 2, 3, 1)
    wkv, s_fin = _scan([to_scan(a) for a in (r, d, k, nkk, ka, v)], state_wkv.transpose(1, 3, 2, 0), tt=t_new)
    wkv = wkv.transpose(0, 3, 1, 2).reshape(1, n, D_RW)
    s_fin = s_fin.transpose(3, 0, 2, 1)

    kf_b, vf_b, logf_b = bmajor(kf), bmajor(vf), bmajor(logf)
    head_mask = (jnp.arange(D_FX)[None, :] // HEAD_DIM == jnp.arange(H_FX)[:, None]).astype(BF16)
    qbig = (bmajor(q)[:, :, None, :] * head_mask).reshape(nb, t_new * H_FX, D_FX)
    nk = PACKED_ROWS
    pad_rows = lambda a: jnp.pad(a, ((0, 0), (0, nk - t_new), (0, 0)))
    lf_new = jnp.pad(logf_b.transpose(0, 2, 1), ((0, 0), (0, 0), (0, LANES - t_new)))
    n_pool = cache_k.shape[0]
    pages_t = lambda c: c.transpose(0, 2, 3, 1).reshape(n_pool, D_FX, PAGE_SIZE)
    fx = _dec_attn(page_table, qbig, pad_rows(kf_b), pad_rows(vf_b), lf_new,
                   pages_t(cache_k), pages_t(cache_v), cache_logf.transpose(0, 2, 1),
                   jnp.tile(head_mask.astype(F32), (t_new, 1)), wts['fx_norm_g'], pc=32, t_new=t_new)
    x1 = _mix(x_tm, wkv, bonus, g, tmajor(fx), wts, tm=nb)
    conv0 = state_conv.transpose(1, 0, 2).reshape(1, (CONV_W - 1) * nb, D_FF)
    y, conv_hist = _ffn(x1, tmajor(pe), conv0, wts, tm=nb, sh=nb)
    heads = lambda a: a.reshape(nb, t_new, H_FX, HEAD_DIM)
    conv_new = conv_hist.reshape(CONV_W - 1, nb, D_FF).transpose(1, 0, 2)
    return bmajor(y[0]), (heads(kf_b), heads(vf_b), logf_b, s_fin, shift_hist[0], conv_new)


def kernel(x_prompt, x_sample, cache_k, cache_v, cache_logf, state_wkv, state_shift, state_conv, page_table, p_prompt, p_sample, ln_in_g, ln_in_b, w_in, b_f, rw_mu, rw_w0, rw_w2, rw_a0, rw_a2, rw_g2, rw_kk, rw_ka, rw_rk, rw_gn_g, rw_gn_b, fx_norm_g, w_o, ln1_g, ln1_b, w_up, conv_w, conv_b, w_down, ln2_g, ln2_b, w_ple, w_pg, b_pg):
    assert w_in.shape[0] == 1, "single-layer model"
    wts = _prep_weights(ln_in_g, ln_in_b, w_in[0], b_f[0], rw_mu[0], rw_w0[0], rw_w2[0], rw_a0[0], rw_a2[0],
                        rw_g2[0], rw_kk[0], rw_ka[0], rw_rk[0].reshape(-1), rw_gn_g[0], rw_gn_b[0], fx_norm_g[0],
                        w_o[0], ln1_g[0], ln1_b[0], w_up[0], conv_w[0], conv_b[0], w_down[0], ln2_g[0], ln2_b[0],
                        w_ple[0], w_pg[0], b_pg[0])
    yp, st_p = _prompt_layer(x_prompt, p_prompt[0], wts)
    ys, st_s = _sample_layer(x_sample, p_sample[0], state_shift[0], state_wkv[0], state_conv[0],
                             cache_k[0], cache_v[0], cache_logf[0], page_table, wts)
    return (yp, ys) + tuple(a[None] for a in st_p) + tuple(a[None] for a in st_s)
```

```python
import functools
import math

import jax
import jax.numpy as jnp
import numpy as np
from jax import lax
from jax.experimental import pallas as pl
from jax.experimental.pallas import tpu as pltpu

D_MODEL = 1024
HEAD_DIM = 64
D_RW = 512
H_RW = 8
D_FX = 512
H_FX = 8
R_DECAY = 64
R_ICL = 64
R_GATE = 128
C_RW = 3 * D_RW + R_DECAY + R_ICL + R_GATE
D_FF = 2816
CONV_W = 3
PLE_DIM = 256
PAGE_SIZE = 128
LN_EPS = 1e-5
GN_EPS = 64e-5
RMS_EPS = 1e-6
SCALE = HEAD_DIM ** -0.5
ALPHA = 2.0 ** 0.25
LOG2E = math.log2(math.e)

LANES = 128
SUBLANES = 8
PACKED_ROWS = 16
NEG_BIG = -1e30
VMEM_LIMIT = 56 * 1024 * 1024
EXTRA_ROWS = PACKED_ROWS
PIECES = 3

BF16 = jnp.bfloat16
F32 = jnp.float32


def _dot(a, b):
    return jnp.dot(a, b, preferred_element_type=F32)


def _dot_nt(a, b):
    return lax.dot_general(a, b, (((1,), (1,)), ((), ())), preferred_element_type=F32)


def _split3(x):
    hi = x.astype(BF16).astype(F32)
    r1 = x - hi
    mid = r1.astype(BF16).astype(F32)
    lo = (r1 - mid).astype(BF16).astype(F32)
    return hi, mid, lo


def _split2(x):
    hi = x.astype(BF16)
    lo = (x - hi.astype(F32)).astype(BF16)
    return hi, lo


def _segsum(x, bd):
    hi, lo = _split2(x)
    return _dot(hi, bd) + _dot(lo, bd)


def _rows_times(x, mat):
    n = x.shape[0]
    pieces = list(_split3(x))
    pad = (-PIECES * n) % PACKED_ROWS
    if pad:
        pieces.append(jnp.zeros((pad, x.shape[1]), F32))
    r = _dot(jnp.concatenate(pieces, axis=0).astype(BF16), mat)
    return r[0:n] + r[n:2 * n] + r[2 * n:3 * n]


def _layer_norm(x, g, b):
    mu = jnp.mean(x, -1, keepdims=True)
    xc = x - mu
    var = jnp.mean(xc * xc, -1, keepdims=True)
    return xc * lax.rsqrt(var + LN_EPS) * g + b


def _softplus(y):
    return jnp.maximum(y, 0.0) + jnp.log1p(jnp.exp(-jnp.abs(y)))


def _const_spec(shape):
    nd = len(shape)
    return pl.BlockSpec(shape, lambda *_: (0,) * nd, pipeline_mode=pl.Buffered(1))


def _params(sem):
    return pltpu.CompilerParams(dimension_semantics=sem, vmem_limit_bytes=VMEM_LIMIT)


_PROJ_COMMON = ['ln_in_g', 'ln_in_b', 'w_rw', 'rw_mu', 'rw_w0', 'rw_w2', 'rw_a0', 'rw_a2', 'rw_g2', 'rw_kk',
                'rw_ka', 'rw_rk', 'bd', 'w_k', 'w_f', 'b_f']
_PROJ_PROMPT = ['w_qt', 'w_kt', 'w_vt', 'w_ft', 'b_ft', 'place_q', 'ones_q', 'place_k', 'ones_k']
_PROJ_SAMPLE = ['w_q', 'w_v']


def _proj_kernel(*refs, tm, sh, hist, prompt, q_scale):
    nc = len(_PROJ_COMMON)
    x_ref, hist0_ref = refs[:2]
    (lng, lnb, w_rw, mu, w0, w2, a0, a2, g2, kkw, kaw, rk, bd, w_k, w_f, b_f) = refs[2:2 + nc]
    pos = 2 + nc
    if prompt:
        (w_qt, w_kt, w_vt, w_ft, b_ft, place_q, ones_q, place_k, ones_k, tril, triu) = refs[pos:pos + 11]
        pos += 11
    else:
        w_q, w_v = refs[pos:pos + 2]
        pos += 2
    (r_o, d_o, k_o, v_o, nkk_o, ka_o, g_o, bonus_o) = refs[pos:pos + 8]
    pos += 8
    if prompt:
        (qt_o, eqt_o, kr_o, ek_o, kt_o, vt_o, logft_o, shift_o, zext, carry_r, carry_t) = refs[pos:]
    else:
        (q_o, kf_o, vf_o, logf_o, shift_o, zext) = refs[pos:]

    t = pl.program_id(1)

    @pl.when(t == 0)
    def _():
        zext[0:hist, :] = hist0_ref[0]
        if prompt:
            carry_r[...] = jnp.zeros_like(carry_r)
            carry_t[...] = jnp.zeros_like(carry_t)

    xn = _layer_norm(x_ref[0], lng[...], lnb[...])
    xb = xn.astype(BF16)

    zr = _dot(xb, w_rw[...])
    zext[hist:hist + tm, :] = zr
    prev = zext[hist - sh:hist - sh + tm, :]
    zs = zr + (prev - zr) * mu[...]
    new_hist = zext[tm:tm + hist, :]
    shift_o[0] = new_hist
    zext[0:hist, :] = new_hist

    r = zs[:, 0:D_RW]
    k = zs[:, D_RW:2 * D_RW]
    v = zs[:, 2 * D_RW:3 * D_RW]
    lw = zs[:, 3 * D_RW:3 * D_RW + R_DECAY + R_ICL]
    xg = zs[:, 3 * D_RW + R_DECAY + R_ICL:C_RW]
    w_log = -_softplus(-(w0[...] + _dot(jnp.tanh(lw).astype(BF16), w2[...]))) - 0.5
    a = jax.nn.sigmoid(a0[...] + _dot(lw.astype(BF16), a2[...]))
    g = _dot(jax.nn.sigmoid(xg).astype(BF16), g2[...])
    kkr = k * kkw[...]
    kk = kkr / jnp.maximum(jnp.sqrt(_segsum(kkr * kkr, bd[...])), 1e-12)
    k_h = k * (1.0 + (a - 1.0) * kaw[...])
    r_o[0] = r
    d_o[0] = jnp.exp(-jnp.exp(w_log))
    k_o[0] = k_h
    v_o[0] = v
    nkk_o[0] = -kk
    ka_o[0] = kk * a
    g_o[0] = g
    bonus_o[0] = _segsum(r * k_h * rk[...], bd[...]) * v

    logf = -_softplus(-(_dot(xb, w_f[...]) + b_f[...]))
    kf = _dot(xb, w_k[...])
    if not prompt:
        q_o[0] = (_dot(xb, w_q[...]) * q_scale).astype(BF16)
        kf_o[0] = kf
        vf_o[0] = _dot(xb, w_v[...])
        logf_o[0] = logf[:, 0:H_FX]
        return

    qt_o[0] = (_dot_nt(w_qt[...], xb) * q_scale).astype(BF16)
    kr_o[0] = kf.astype(BF16)
    kt_o[0] = _dot_nt(w_kt[...], xb)
    vt_o[0] = _dot_nt(w_vt[...], xb)
    logft = -_softplus(-(_dot_nt(w_ft[...], xb)[0:H_FX] + b_ft[...]))
    logft_o[0] = logft
    cum_t = _rows_times(logft, triu[...]) + carry_t[...]
    carry_t[...] = cum_t[:, tm - 1:tm]
    pieces_t = list(_split3(cum_t * LOG2E)) + [jnp.zeros((SUBLANES, tm), F32)]
    eqt = _dot(place_q[...], jnp.concatenate(pieces_t, axis=0).astype(BF16)) + ones_q[...]
    eqt_o[0] = eqt.astype(BF16)

    hi, mid, lo = _split3(logf)
    trib = tril[...]
    cum_r = (_dot(trib, hi.astype(BF16)) + _dot(trib, mid.astype(BF16)) + _dot(trib, lo.astype(BF16))
             + carry_r[...])
    carry_r[...] = cum_r[tm - 1:tm, :]
    pieces_r = jnp.concatenate(_split3(cum_r * LOG2E), axis=1).astype(BF16)
    ek_o[0] = (_dot(pieces_r, place_k[...]) + ones_k[...]).astype(BF16)


def _bias_placement():
    pairs = H_FX // 2
    place_q = np.zeros((pairs * EXTRA_ROWS, (PIECES + 1) * SUBLANES), np.float32)
    ones_q = np.zeros((pairs * EXTRA_ROWS, 1), np.float32)
    place_k = np.zeros((PIECES * LANES, pairs * LANES), np.float32)
    ones_k = np.zeros((1, pairs * LANES), np.float32)
    for h in range(H_FX):
        p, o = h // 2, (h % 2) * 2 * PIECES
        for piece in range(PIECES):
            place_q[p * EXTRA_ROWS + o + piece, piece * SUBLANES + h] = 1.0
            ones_q[p * EXTRA_ROWS + o + PIECES + piece, 0] = 1.0
            ones_k[0, p * LANES + o + piece] = 1.0
            place_k[piece * LANES + h, p * LANES + o + PIECES + piece] = -1.0
    return (jnp.asarray(place_q, BF16), jnp.asarray(ones_q), jnp.asarray(place_k, BF16), jnp.asarray(ones_k))


def _proj(x, hist0, wts, *, tm, sh, prompt):
    nb, n, _ = x.shape
    hist = hist0.shape[1]
    consts = [wts[nm] for nm in _PROJ_COMMON + (_PROJ_PROMPT if prompt else _PROJ_SAMPLE)]
    if prompt:
        tril = jnp.tril(jnp.ones((tm, tm), BF16))
        consts += [tril, tril.T]
    row = lambda c, dt=F32: (jax.ShapeDtypeStruct((nb, n, c), dt), pl.BlockSpec((1, tm, c), lambda b, t: (b, t, 0)))
    col = lambda c, dt=F32: (jax.ShapeDtypeStruct((nb, c, n), dt), pl.BlockSpec((1, c, tm), lambda b, t: (b, 0, t)))
    outs = [row(D_RW) for _ in range(8)]
    scratch = [pltpu.VMEM((hist + tm, C_RW), F32)]
    if prompt:
        outs += [col(D_FX, BF16), col(H_FX // 2 * EXTRA_ROWS, BF16), row(D_FX, BF16), row(D_FX, BF16),
                 col(D_FX), col(D_FX), col(H_FX)]
        scratch += [pltpu.VMEM((1, LANES), F32), pltpu.VMEM((H_FX, 1), F32)]
    else:
        outs += [row(D_FX, BF16), row(D_FX), row(D_FX), row(H_FX)]
    outs.append((jax.ShapeDtypeStruct((nb, hist, C_RW), F32), pl.BlockSpec((1, hist, C_RW), lambda b, t: (b, 0, 0))))
    return pl.pallas_call(
        functools.partial(_proj_kernel, tm=tm, sh=sh, hist=hist, prompt=prompt,
                          q_scale=SCALE * LOG2E if prompt else SCALE),
        grid=(nb, n // tm),
        in_specs=[pl.BlockSpec((1, tm, D_MODEL), lambda b, t: (b, t, 0)),
                  pl.BlockSpec((1, hist, C_RW), lambda b, t: (b, 0, 0))]
                 + [_const_spec(c.shape) for c in consts],
        out_specs=[o[1] for o in outs],
        out_shape=[o[0] for o in outs],
        scratch_shapes=scratch,
        compiler_params=_params(("parallel", "arbitrary")),
        name="proj",
    )(x, hist0, *consts)


SCAN_J_BLOCK = 32


def _scan_kernel(r_ref, d_ref, k_ref, nkk_ref, ka_ref, v_ref, s0_ref, out_ref, s_ref, *, tt):
    @pl.when(pl.program_id(1) == 0)
    def _():
        s_ref[...] = s0_ref[...]

    zero = jnp.zeros((HEAD_DIM, LANES), F32)

    def step(i, carry):
        row = lambda ref, j: ref[i, pl.ds(j, 1), :]

        def contract(jb, acc):
            a0, a1 = acc
            for u in range(0, SCAN_J_BLOCK, 2):
                j = jb * SCAN_J_BLOCK + u
                a0 = a0 + s_ref[j] * row(nkk_ref, j)
                a1 = a1 + s_ref[j + 1] * row(nkk_ref, j + 1)
            return a0, a1

        sa0, sa1 = lax.fori_loop(0, HEAD_DIM // SCAN_J_BLOCK, contract, (zero, zero))
        sa = sa0 + sa1
        v_t = v_ref[i]

        def update(jb, acc):
            o0, o1 = acc
            for u in range(SCAN_J_BLOCK):
                j = jb * SCAN_J_BLOCK + u
                sj = s_ref[j] * row(d_ref, j) + sa * row(ka_ref, j) + v_t * row(k_ref, j)
                s_ref[j] = sj
                if u % 2 == 0:
                    o0 = o0 + sj * row(r_ref, j)
                else:
                    o1 = o1 + sj * row(r_ref, j)
            return o0, o1

        o0, o1 = lax.fori_loop(0, HEAD_DIM // SCAN_J_BLOCK, update, (zero, zero))
        out_ref[i] = o0 + o1
        return carry

    lax.fori_loop(0, tt, step, 0)


def _scan(xs, s0, *, tt):
    steps, groups = xs[0].shape[:2]
    seq = pl.BlockSpec((tt, None, HEAD_DIM, LANES), lambda g, t: (t, g, 0, 0))
    st = pl.BlockSpec((None, HEAD_DIM, HEAD_DIM, LANES), lambda g, t: (g, 0, 0, 0))
    return pl.pallas_call(
        functools.partial(_scan_kernel, tt=tt),
        grid=(groups, steps // tt),
        in_specs=[seq] * 6 + [st],
        out_specs=[seq, st],
        out_shape=[jax.ShapeDtypeStruct(xs[0].shape, F32), jax.ShapeDtypeStruct(s0.shape, F32)],
        compiler_params=_params(("parallel", "arbitrary")),
        name="scan",
    )(*xs, s0)


def _attn_kernel(qi_ref, ki_ref, qt_ref, eqt_ref, kr_ref, ek_ref, vt_ref, g_ref, o_ref,
                 qa0, qa1, m0_s, m1_s, l0_s, l1_s, acc, *, tq, tk):
    step = pl.program_id(2)
    qi = qi_ref[step]
    ki = ki_ref[step]
    sub = lax.broadcasted_iota(jnp.int32, (LANES, 1), 0)
    head0 = sub < HEAD_DIM

    @pl.when(ki == 0)
    def _():
        q2 = qt_ref[0]
        eq = eqt_ref[0]
        zero = jnp.zeros_like(q2)
        ezero = jnp.zeros_like(eq)
        first = lax.broadcasted_iota(jnp.int32, (EXTRA_ROWS, 1), 0) < 2 * PIECES
        pad = jnp.zeros((LANES - EXTRA_ROWS, tq), BF16)
        qa0[...] = jnp.concatenate([jnp.where(head0, q2, zero), jnp.where(first, eq, ezero), pad], axis=0)
        qa1[...] = jnp.concatenate([jnp.where(head0, zero, q2), jnp.where(first, ezero, eq), pad], axis=0)
        m0_s[...] = jnp.full_like(m0_s, NEG_BIG)
        m1_s[...] = jnp.full_like(m1_s, NEG_BIG)
        l0_s[...] = jnp.zeros_like(l0_s)
        l1_s[...] = jnp.zeros_like(l1_s)
        acc[...] = jnp.zeros_like(acc)

    def update(masked):
        kaug = jnp.concatenate([kr_ref[0], ek_ref[0]], axis=1)
        vt2 = vt_ref[0]
        vz = jnp.zeros_like(vt2)
        pad = jnp.zeros((LANES - EXTRA_ROWS, tk), BF16)
        ones = jnp.where(lax.broadcasted_iota(jnp.int32, (EXTRA_ROWS, tk), 0) == 0, 1.0, 0.0).astype(BF16)
        vaug = [jnp.concatenate([jnp.where(head0, vt2, vz).astype(BF16), ones, pad], axis=0),
                jnp.concatenate([jnp.where(head0, vz, vt2).astype(BF16), ones, pad], axis=0)]
        scores = [_dot(kaug, qa0[...]), _dot(kaug, qa1[...])]
        if masked:
            causal = (lax.broadcasted_iota(jnp.int32, (tk, tq), 0)
                      <= lax.broadcasted_iota(jnp.int32, (tk, tq), 1))
            scores = [jnp.where(causal, s, NEG_BIG) for s in scores]
        corr = []
        outs = []
        for s, m_s in zip(scores, (m0_s, m1_s)):
            m_old = m_s[...]
            m_new = jnp.maximum(m_old, jnp.max(s, axis=0, keepdims=True))
            m_s[...] = m_new
            corr.append(jnp.exp2(m_old - m_new))
            outs.append(jnp.exp2(s - m_new).astype(BF16))
        outs = [_dot(va, p) for va, p in zip(vaug, outs)]
        l0_s[...] = l0_s[...] * corr[0] + outs[0][LANES:LANES + SUBLANES]
        l1_s[...] = l1_s[...] * corr[1] + outs[1][LANES:LANES + SUBLANES]
        acc[...] = acc[...] * jnp.where(head0, corr[0], corr[1]) + outs[0][:LANES] + outs[1][:LANES]

    @pl.when(ki < qi)
    def _():
        update(False)

    @pl.when(ki == qi)
    def _():
        update(True)
        ot = acc[...] * jnp.where(head0, 1.0 / l0_s[0:1, :], 1.0 / l1_s[0:1, :])
        o = ot.T
        o2 = o * o
        lane0 = lax.broadcasted_iota(jnp.int32, (1, LANES), 1) < HEAD_DIM
        ms0 = jnp.sum(jnp.where(lane0, o2, 0.0), -1, keepdims=True) * (1.0 / HEAD_DIM)
        ms1 = jnp.sum(jnp.where(lane0, 0.0, o2), -1, keepdims=True) * (1.0 / HEAD_DIM)
        rn = jnp.where(lane0, lax.rsqrt(ms0 + RMS_EPS), lax.rsqrt(ms1 + RMS_EPS))
        o_ref[0] = (o * rn * g_ref[...]).astype(BF16)


def _attn(qt, eqt, kr, ek, vt, gain, *, tq):
    nb, _, s = qt.shape
    tk = tq
    nq = s // tq
    pairs = D_FX // LANES
    qi_tab = jnp.asarray([qi for qi in range(nq) for _ in range(qi + 1)], jnp.int32)
    ki_tab = jnp.asarray([ki for qi in range(nq) for ki in range(qi + 1)], jnp.int32)
    q_col = lambda b, p, s_, qt_, kt_: (b, p, qt_[s_])
    k_row = lambda b, p, s_, qt_, kt_: (b, kt_[s_], p)
    return pl.pallas_call(
        functools.partial(_attn_kernel, tq=tq, tk=tk),
        grid_spec=pltpu.PrefetchScalarGridSpec(
            num_scalar_prefetch=2,
            grid=(nb, pairs, int(qi_tab.shape[0])),
            in_specs=[pl.BlockSpec((1, LANES, tq), q_col),
                      pl.BlockSpec((1, EXTRA_ROWS, tq), q_col),
                      pl.BlockSpec((1, tk, LANES), k_row),
                      pl.BlockSpec((1, tk, LANES), k_row),
                      pl.BlockSpec((1, LANES, tk), lambda b, p, s_, qt_, kt_: (b, p, kt_[s_])),
                      pl.BlockSpec((1, LANES), lambda b, p, s_, qt_, kt_: (0, p))],
            out_specs=pl.BlockSpec((1, tq, LANES), lambda b, p, s_, qt_, kt_: (b, qt_[s_], p)),
            scratch_shapes=[pltpu.VMEM((2 * LANES, tq), BF16), pltpu.VMEM((2 * LANES, tq), BF16),
                            pltpu.VMEM((1, tq), F32), pltpu.VMEM((1, tq), F32),
                            pltpu.VMEM((SUBLANES, tq), F32), pltpu.VMEM((SUBLANES, tq), F32),
                            pltpu.VMEM((LANES, tq), F32)]),
        out_shape=jax.ShapeDtypeStruct((nb, s, D_FX), BF16),
        compiler_params=_params(("parallel", "parallel", "arbitrary")),
        name="attn",
    )(qi_tab, ki_tab, qt, eqt, kr, ek, vt, gain)


def _dec_attn_kernel(pt_ref, q_ref, kn_ref, vn_ref, lfn_ref, gt_ref, hm_ref, g_ref, *rest, pc, nc, t_new):
    kp = rest[0:pc]
    vp = rest[pc:2 * pc]
    lp = rest[2 * pc:3 * pc]
    o_ref, m_s, l_s, acc, carry = rest[3 * pc:]
    c = pl.program_id(1)
    q = q_ref[0]
    per_query = lambda bias: jnp.concatenate([bias] * t_new, axis=0)

    def update(s, pv_of):
        m_old = m_s[...]
        m_new = jnp.maximum(m_old, jnp.max(s, -1, keepdims=True))
        corr = jnp.exp(m_old - m_new)
        p = jnp.exp(s - m_new)
        l_s[...] = l_s[...] * corr + jnp.sum(p, -1, keepdims=True)
        m_s[...] = m_new
        acc[...] = acc[...] * corr + pv_of(p.astype(BF16))

    @pl.when(c == 0)
    def _():
        m_s[...] = jnp.full_like(m_s, NEG_BIG)
        l_s[...] = jnp.zeros_like(l_s)
        acc[...] = jnp.zeros_like(acc)
        lf = lfn_ref[0]
        nk = kn_ref.shape[1]
        s = _dot_nt(q, kn_ref[0].astype(BF16)) + per_query(_rows_times(lf, gt_ref[...])[:, 0:nk])
        key_t = lax.broadcasted_iota(jnp.int32, s.shape, 1)
        qry_t = lax.broadcasted_iota(jnp.int32, s.shape, 0) // H_FX
        update(jnp.where(key_t <= qry_t, s, NEG_BIG), lambda p: _dot(p, vn_ref[0].astype(BF16)))
        carry[...] = jnp.sum(lf, -1, keepdims=True)

    lf_all = jnp.concatenate([lp[i][...] for i in range(pc)], axis=0)
    within = _rows_times(lf_all, gt_ref[...])
    totals = jnp.sum(lf_all, -1, keepdims=True)
    later = carry[...]
    scores = []
    for i in range(pc):
        rows = slice(i * H_FX, (i + 1) * H_FX)
        scores.append(_dot(q, kp[i][...].astype(BF16)) + per_query(within[rows] + later))
        later = later + totals[rows]
    carry[...] = later

    def pages_pv(p):
        out = _dot_nt(p[:, 0:PAGE_SIZE], vp[0][...].astype(BF16))
        for i in range(1, pc):
            out = out + _dot_nt(p[:, i * PAGE_SIZE:(i + 1) * PAGE_SIZE], vp[i][...].astype(BF16))
        return out

    update(jnp.concatenate(scores, axis=1), pages_pv)

    @pl.when(c == nc - 1)
    def _():
        o = acc[...] * (1.0 / l_s[...]) * hm_ref[...]
        ms = jnp.sum(o * o, -1, keepdims=True) * (1.0 / HEAD_DIM)
        o = o * lax.rsqrt(ms + RMS_EPS)
        per_t = [jnp.sum(o[t * H_FX:(t + 1) * H_FX], axis=0, keepdims=True) for t in range(t_new)]
        o_ref[0] = (jnp.concatenate(per_t, axis=0) * g_ref[...]).astype(BF16)


def _dec_attn(page_table, qbig, k_new, v_new, lf_new, cache_kt, cache_vt, cache_lft, head_mask, gain, *, pc, t_new):
    nb, n_pages = page_table.shape
    nc = n_pages // pc
    nk = k_new.shape[1]
    gt = jnp.triu(jnp.ones((PAGE_SIZE, PAGE_SIZE), BF16), 1).T
    rows = t_new * H_FX

    def page_spec(shape, i):
        return pl.BlockSpec((None,) + shape, lambda b, c, pt: (pt[b, n_pages - 1 - (c * pc + i)], 0, 0))

    per_b = lambda shape: pl.BlockSpec((1,) + shape, lambda b, c, pt: (b, 0, 0))
    const = lambda shape: pl.BlockSpec(shape, lambda b, c, pt: (0, 0))
    in_specs = ([per_b((rows, D_FX)), per_b((nk, D_FX)), per_b((nk, D_FX)), per_b((H_FX, LANES)),
                 const((PAGE_SIZE, PAGE_SIZE)), const((rows, D_FX)), const((1, D_FX))]
                + [page_spec((D_FX, PAGE_SIZE), i) for i in range(pc)]
                + [page_spec((D_FX, PAGE_SIZE), i) for i in range(pc)]
                + [page_spec((H_FX, PAGE_SIZE), i) for i in range(pc)])
    return pl.pallas_call(
        functools.partial(_dec_attn_kernel, pc=pc, nc=nc, t_new=t_new),
        grid_spec=pltpu.PrefetchScalarGridSpec(
            num_scalar_prefetch=1,
            grid=(nb, nc),
            in_specs=in_specs,
            out_specs=pl.BlockSpec((1, t_new, D_FX), lambda b, c, pt: (b, 0, 0)),
            scratch_shapes=[pltpu.VMEM((rows, 1), F32), pltpu.VMEM((rows, 1), F32),
                            pltpu.VMEM((rows, D_FX), F32), pltpu.VMEM((H_FX, 1), F32)]),
        out_shape=jax.ShapeDtypeStruct((nb, t_new, D_FX), BF16),
        compiler_params=_params(("parallel", "arbitrary")),
        name="dec_attn",
    )(page_table, qbig, k_new, v_new, lf_new, gt, head_mask, gain,
      *([cache_kt] * pc), *([cache_vt] * pc), *([cache_lft] * pc))


FF_CHUNK = D_FF // 2
_TAIL_CONSTS = ['ln_in_g', 'ln_in_b', 'rw_gn_g', 'rw_gn_b', 'bd', 'wo_rw', 'wo_fx', 'ln1_g', 'ln1_b',
                'w_up_a', 'w_up_b', 'conv_w', 'conv_b', 'w_down', 'ln2_g', 'ln2_b', 'w_pg', 'b_pg', 'w_ple']


def _tail_kernel(x_ref, wkv_ref, bonus_ref, g_ref, fx_ref, pe_ref, hist0_ref,
                 lng, lnb, gng, gnb, bd, wo_rw, wo_fx, l1g, l1b,
                 wa, wb, cw, cb, wd, l2g, l2b, wpg, bpg, wple, y_ref, hist_o, aext, *, tm, sh, hist):
    @pl.when(pl.program_id(1) == 0)
    def _():
        aext[0:hist, :] = hist0_ref[0]

    xn = _layer_norm(x_ref[0], lng[...], lnb[...])
    wkv = wkv_ref[0]
    mean = _segsum(wkv, bd[...]) * (1.0 / HEAD_DIM)
    wc = wkv - mean
    var = _segsum(wc * wc, bd[...]) * (1.0 / HEAD_DIM)
    wn = wc * lax.rsqrt(var + GN_EPS) * gng[...] + gnb[...]
    rw_y = (wn + bonus_ref[0]) * g_ref[0]
    mix = _dot(rw_y.astype(BF16), wo_rw[...]) + _dot(fx_ref[0], wo_fx[...])
    x1 = _layer_norm(ALPHA * xn + mix, l1g[...], l1b[...])
    xb = x1.astype(BF16)
    y = jnp.zeros((tm, D_MODEL), F32)
    for c0 in range(0, D_FF, FF_CHUNK):
        cs = slice(c0, c0 + FF_CHUNK)
        a = _dot(xb, wa[:, cs])
        aext[hist:hist + tm, cs] = a
        p1 = aext[hist - sh:hist - sh + tm, cs]
        p2 = aext[hist - 2 * sh:hist - 2 * sh + tm, cs]
        conv = cb[:, cs] + p2 * cw[0:1, cs] + p1 * cw[1:2, cs] + a * cw[2:3, cs]
        gelu = 0.5 * conv * (1.0 + lax.erf(conv * (2.0 ** -0.5)))
        h = gelu * _dot(xb, wb[:, cs])
        y = y + _dot(h.astype(BF16), wd[cs, :])
    new_hist = aext[tm:tm + hist, :]
    hist_o[0] = new_hist
    aext[0:hist, :] = new_hist
    x2 = _layer_norm(ALPHA * x1 + y, l2g[...], l2b[...])
    gate = jax.nn.sigmoid(_dot(x2.astype(BF16), wpg[...]) + bpg[...])
    y_ref[0] = x2 + gate * _dot(pe_ref[0].astype(BF16), wple[...])


def _tail(x, wkv, bonus, g, fx, pe, hist0, wts, *, tm, sh):
    nb, n, _ = x.shape
    hist = hist0.shape[1]
    consts = [wts[nm] for nm in _TAIL_CONSTS]
    row = lambda c: pl.BlockSpec((1, tm, c), lambda b, t: (b, t, 0))
    hist_spec = pl.BlockSpec((1, hist, D_FF), lambda b, t: (b, 0, 0))
    return pl.pallas_call(
        functools.partial(_tail_kernel, tm=tm, sh=sh, hist=hist),
        grid=(nb, n // tm),
        in_specs=[row(D_MODEL), row(D_RW), row(D_RW), row(D_RW), row(D_FX), row(PLE_DIM), hist_spec]
                 + [_const_spec(c.shape) for c in consts],
        out_specs=[row(D_MODEL), hist_spec],
        out_shape=[jax.ShapeDtypeStruct((nb, n, D_MODEL), F32), jax.ShapeDtypeStruct((nb, hist, D_FF), F32)],
        scratch_shapes=[pltpu.VMEM((hist + tm, D_FF), F32)],
        compiler_params=_params(("parallel", "arbitrary")),
        name="tail",
    )(x, wkv, bonus, g, fx, pe, hist0, *consts)


def _prep_weights(ln_in_g, ln_in_b, w_in, b_f, rw_mu, rw_w0, rw_w2, rw_a0, rw_a2, rw_g2, rw_kk, rw_ka, rw_rk,
                  rw_gn_g, rw_gn_b, fx_norm_g, w_o, ln1_g, ln1_b, w_up, conv_w, conv_b, w_down, ln2_g, ln2_b,
                  w_ple, w_pg, b_pg):
    r2 = lambda a: a.reshape(1, -1)
    bf = lambda a: a.astype(BF16)
    lora_rows = R_DECAY + R_ICL
    seg = jnp.arange(D_RW) // HEAD_DIM
    fx0 = C_RW
    w_f = w_in[:, fx0 + 3 * D_FX:]
    place_q, ones_q, place_k, ones_k = _bias_placement()
    return dict(
        ln_in_g=r2(ln_in_g), ln_in_b=r2(ln_in_b),
        w_rw=bf(w_in[:, :C_RW]),
        w_q=bf(w_in[:, fx0:fx0 + D_FX]),
        w_k=bf(w_in[:, fx0 + D_FX:fx0 + 2 * D_FX]), w_v=bf(w_in[:, fx0 + 2 * D_FX:fx0 + 3 * D_FX]),
        w_qt=bf(w_in[:, fx0:fx0 + D_FX].T), w_kt=bf(w_in[:, fx0 + D_FX:fx0 + 2 * D_FX].T), w_vt=bf(w_in[:, fx0 + 2 * D_FX:fx0 + 3 * D_FX].T),
        w_f=bf(jnp.pad(w_f, ((0, 0), (0, LANES - H_FX)))),
        b_f=jnp.pad(b_f, (0, LANES - H_FX)).reshape(1, LANES),
        w_ft=bf(jnp.pad(w_f.T, ((0, PACKED_ROWS - H_FX), (0, 0)))), b_ft=b_f.reshape(H_FX, 1),
        place_q=place_q, ones_q=ones_q, place_k=place_k, ones_k=ones_k,
        rw_mu=r2(rw_mu), rw_w0=r2(rw_w0), rw_a0=r2(rw_a0),
        rw_w2=bf(jnp.pad(rw_w2, ((0, lora_rows - R_DECAY), (0, 0)))),
        rw_a2=bf(jnp.pad(rw_a2, ((lora_rows - R_ICL, 0), (0, 0)))),
        rw_g2=bf(rw_g2), rw_kk=r2(rw_kk), rw_ka=r2(rw_ka), rw_rk=r2(rw_rk),
        bd=(seg[:, None] == seg[None, :]).astype(BF16),
        rw_gn_g=r2(rw_gn_g), rw_gn_b=r2(rw_gn_b), fx_norm_g=r2(fx_norm_g),
        wo_rw=bf(w_o[:D_RW]), wo_fx=bf(w_o[D_RW:]), ln1_g=r2(ln1_g), ln1_b=r2(ln1_b),
        w_up_a=bf(w_up[:, :D_FF]), w_up_b=bf(w_up[:, D_FF:]), conv_w=conv_w, conv_b=r2(conv_b),
        w_down=bf(w_down), ln2_g=r2(ln2_g), ln2_b=r2(ln2_b), w_ple=bf(w_ple), w_pg=bf(w_pg), b_pg=r2(b_pg))


def _prompt_layer(x, pe, wts):
    nb, s, _ = x.shape
    hist = SUBLANES
    (r, d, k, v, nkk, ka, g, bonus, qt, eqt, kr, ek, kt, vt, logft, shift_hist) = _proj(
        x, jnp.zeros((nb, hist, C_RW), F32), wts, tm=256, sh=1, prompt=True)
    to_scan = lambda a: a.reshape(nb, s, H_RW, HEAD_DIM).transpose(1, 3, 0, 2).reshape(s, 1, HEAD_DIM, nb * H_RW)
    wkv, s_fin = _scan([to_scan(a) for a in (r, d, k, nkk, ka, v)],
                       jnp.zeros((1, HEAD_DIM, HEAD_DIM, nb * H_RW), F32), tt=16)
    wkv = wkv.reshape(s, HEAD_DIM, nb, H_RW).transpose(2, 0, 3, 1).reshape(nb, s, D_RW)
    s_fin = s_fin.reshape(HEAD_DIM, HEAD_DIM, nb, H_RW).transpose(2, 3, 1, 0)
    fx = _attn(qt, eqt, kr, ek, vt, wts['fx_norm_g'], tq=1024)
    y, conv_hist = _tail(x, wkv, bonus, g, fx, pe, jnp.zeros((nb, hist, D_FF), F32), wts, tm=256, sh=1)
    heads = lambda a: a.reshape(nb, H_FX, HEAD_DIM, s).transpose(0, 3, 1, 2)
    return y, (heads(kt), heads(vt), logft.transpose(0, 2, 1), s_fin, shift_hist[:, hist - 1],
               conv_hist[:, hist - (CONV_W - 1):])


def _sample_layer(x, pe, state_shift, state_wkv, state_conv, cache_k, cache_v, cache_logf, page_table, wts):
    nb, t_new, _ = x.shape
    n = nb * t_new
    tmajor = lambda a: a.transpose(1, 0, 2).reshape(1, n, a.shape[-1])
    bmajor = lambda a: a.reshape(t_new, nb, a.shape[-1]).transpose(1, 0, 2)
    x_tm = tmajor(x)
    (r, d, k, v, nkk, ka, g, bonus, q, kf, vf, logf, shift_hist) = _proj(
        x_tm, state_shift.reshape(1, nb, C_RW), wts, tm=nb, sh=nb, prompt=False)
    to_scan = lambda a: a.reshape(t_new, nb, H_RW, HEAD_DIM).transpose(0, 2, 3, 1)
    wkv, s_fin = _scan([to_scan(a) for a in (r, d, k, nkk, ka, v)], state_wkv.transpose(1, 3, 2, 0), tt=t_new)
    wkv = wkv.transpose(0, 3, 1, 2).reshape(1, n, D_RW)
    s_fin = s_fin.transpose(3, 0, 2, 1)

    kf_b, vf_b, logf_b = bmajor(kf), bmajor(vf), bmajor(logf)
    head_mask = (jnp.arange(D_FX)[None, :] // HEAD_DIM == jnp.arange(H_FX)[:, None]).astype(BF16)
    qbig = (bmajor(q)[:, :, None, :] * head_mask).reshape(nb, t_new * H_FX, D_FX)
    nk = PACKED_ROWS
    pad_rows = lambda a: jnp.pad(a, ((0, 0), (0, nk - t_new), (0, 0)))
    lf_new = jnp.pad(logf_b.transpose(0, 2, 1), ((0, 0), (0, 0), (0, LANES - t_new)))
    n_pool = cache_k.shape[0]
    pages_t = lambda c: c.transpose(0, 2, 3, 1).reshape(n_pool, D_FX, PAGE_SIZE)
    fx = _dec_attn(page_table, qbig, pad_rows(kf_b), pad_rows(vf_b), lf_new,
                   pages_t(cache_k), pages_t(cache_v), cache_logf.transpose(0, 2, 1),
                   jnp.tile(head_mask.astype(F32), (t_new, 1)), wts['fx_norm_g'], pc=32, t_new=t_new)
    conv0 = state_conv.transpose(1, 0, 2).reshape(1, (CONV_W - 1) * nb, D_FF)
    y, conv_hist = _tail(x_tm, wkv, bonus, g, tmajor(fx), tmajor(pe), conv0, wts, tm=nb, sh=nb)
    heads = lambda a: a.reshape(nb, t_new, H_FX, HEAD_DIM)
    conv_new = conv_hist.reshape(CONV_W - 1, nb, D_FF).transpose(1, 0, 2)
    return bmajor(y[0]), (heads(kf_b), heads(vf_b), logf_b, s_fin, shift_hist[0], conv_new)


def kernel(x_prompt, x_sample, cache_k, cache_v, cache_logf, state_wkv, state_shift, state_conv, page_table, p_prompt, p_sample, ln_in_g, ln_in_b, w_in, b_f, rw_mu, rw_w0, rw_w2, rw_a0, rw_a2, rw_g2, rw_kk, rw_ka, rw_rk, rw_gn_g, rw_gn_b, fx_norm_g, w_o, ln1_g, ln1_b, w_up, conv_w, conv_b, w_down, ln2_g, ln2_b, w_ple, w_pg, b_pg):
    assert w_in.shape[0] == 1, "single-layer model"
    wts = _prep_weights(ln_in_g, ln_in_b, w_in[0], b_f[0], rw_mu[0], rw_w0[0], rw_w2[0], rw_a0[0], rw_a2[0],
                        rw_g2[0], rw_kk[0], rw_ka[0], rw_rk[0].reshape(-1), rw_gn_g[0], rw_gn_b[0], fx_norm_g[0],
                        w_o[0], ln1_g[0], ln1_b[0], w_up[0], conv_w[0], conv_b[0], w_down[0], ln2_g[0], ln2_b[0],
                        w_ple[0], w_pg[0], b_pg[0])
    yp, st_p = _prompt_layer(x_prompt, p_prompt[0], wts)
    ys, st_s = _sample_layer(x_sample, p_sample[0], state_shift[0], state_wkv[0], state_conv[0],
                             cache_k[0], cache_v[0], cache_logf[0], page_table, wts)
    return (yp, ys) + tuple(a[None] for a in st_p) + tuple(a[None] for a in st_s)
```

```python
import functools
import math

import jax
import jax.numpy as jnp
import numpy as np
from jax import lax
from jax.experimental import pallas as pl
from jax.experimental.pallas import tpu as pltpu

D_MODEL = 1024
HEAD_DIM = 64
D_RW = 512
H_RW = 8
D_FX = 512
H_FX = 8
R_DECAY = 64
R_ICL = 64
R_GATE = 128
C_RW = 3 * D_RW + R_DECAY + R_ICL + R_GATE
D_FF = 2816
CONV_W = 3
PLE_DIM = 256
PAGE_SIZE = 128
LN_EPS = 1e-5
GN_EPS = 64e-5
RMS_EPS = 1e-6
SCALE = HEAD_DIM ** -0.5
ALPHA = 2.0 ** 0.25
LOG2E = math.log2(math.e)

LANES = 128
SUBLANES = 8
PACKED_ROWS = 16
NEG_BIG = -1e30
VMEM_LIMIT = 56 * 1024 * 1024
EXTRA_ROWS = PACKED_ROWS
PIECES = 3

BF16 = jnp.bfloat16
F32 = jnp.float32


def _dot(a, b):
    return jnp.dot(a, b, preferred_element_type=F32)


def _dot_nt(a, b):
    return lax.dot_general(a, b, (((1,), (1,)), ((), ())), preferred_element_type=F32)


def _split3(x):
    hi = x.astype(BF16).astype(F32)
    r1 = x - hi
    mid = r1.astype(BF16).astype(F32)
    lo = (r1 - mid).astype(BF16).astype(F32)
    return hi, mid, lo


def _segsum(x, bd):
    return _dot(x.astype(BF16), bd)


def _rows_times(x, mat):
    n = x.shape[0]
    pieces = list(_split3(x))
    pad = (-PIECES * n) % PACKED_ROWS
    if pad:
        pieces.append(jnp.zeros((pad, x.shape[1]), F32))
    r = _dot(jnp.concatenate(pieces, axis=0).astype(BF16), mat)
    return r[0:n] + r[n:2 * n] + r[2 * n:3 * n]


def _layer_norm(x, g, b):
    mu = jnp.mean(x, -1, keepdims=True)
    xc = x - mu
    var = jnp.mean(xc * xc, -1, keepdims=True)
    return xc * lax.rsqrt(var + LN_EPS) * g + b


def _softplus(y):
    return jnp.maximum(y, 0.0) + jnp.log1p(jnp.exp(-jnp.abs(y)))


def _const_spec(shape):
    nd = len(shape)
    return pl.BlockSpec(shape, lambda *_: (0,) * nd, pipeline_mode=pl.Buffered(1))


def _params(sem):
    return pltpu.CompilerParams(dimension_semantics=sem, vmem_limit_bytes=VMEM_LIMIT)


_PROJ_COMMON = ['ln_in_g', 'ln_in_b', 'w_rw', 'rw_mu', 'rw_w0', 'rw_w2', 'rw_a0', 'rw_a2', 'rw_g2', 'rw_kk',
                'rw_ka', 'rw_rk', 'bd', 'w_k', 'w_f', 'b_f']
_PROJ_PROMPT = ['w_qt', 'w_kt', 'w_vt', 'w_ft', 'b_ft', 'place_q', 'ones_q', 'place_k', 'ones_k']
_PROJ_SAMPLE = ['w_q', 'w_v']


def _proj_kernel(*refs, tm, sh, hist, prompt, q_scale):
    nc = len(_PROJ_COMMON)
    x_ref, hist0_ref = refs[:2]
    (lng, lnb, w_rw, mu, w0, w2, a0, a2, g2, kkw, kaw, rk, bd, w_k, w_f, b_f) = refs[2:2 + nc]
    pos = 2 + nc
    if prompt:
        (w_qt, w_kt, w_vt, w_ft, b_ft, place_q, ones_q, place_k, ones_k, tril, triu) = refs[pos:pos + 11]
        pos += 11
    else:
        w_q, w_v = refs[pos:pos + 2]
        pos += 2
    (r_o, d_o, k_o, v_o, nkk_o, ka_o, g_o, bonus_o) = refs[pos:pos + 8]
    pos += 8
    if prompt:
        (qt_o, eqt_o, kr_o, ek_o, kt_o, vt_o, logft_o, shift_o, zext, carry_r, carry_t) = refs[pos:]
    else:
        (q_o, kf_o, vf_o, logf_o, shift_o, zext) = refs[pos:]

    t = pl.program_id(1)

    @pl.when(t == 0)
    def _():
        zext[0:hist, :] = hist0_ref[0]
        if prompt:
            carry_r[...] = jnp.zeros_like(carry_r)
            carry_t[...] = jnp.zeros_like(carry_t)

    xn = _layer_norm(x_ref[0], lng[...], lnb[...])
    xb = xn.astype(BF16)

    zr = _dot(xb, w_rw[...])
    zext[hist:hist + tm, :] = zr
    prev = zext[hist - sh:hist - sh + tm, :]
    zs = zr + (prev - zr) * mu[...]
    new_hist = zext[tm:tm + hist, :]
    shift_o[0] = new_hist
    zext[0:hist, :] = new_hist

    r = zs[:, 0:D_RW]
    k = zs[:, D_RW:2 * D_RW]
    v = zs[:, 2 * D_RW:3 * D_RW]
    lw = zs[:, 3 * D_RW:3 * D_RW + R_DECAY + R_ICL]
    xg = zs[:, 3 * D_RW + R_DECAY + R_ICL:C_RW]
    w_log = -_softplus(-(w0[...] + _dot(jnp.tanh(lw).astype(BF16), w2[...]))) - 0.5
    a = jax.nn.sigmoid(a0[...] + _dot(lw.astype(BF16), a2[...]))
    g = _dot(jax.nn.sigmoid(xg).astype(BF16), g2[...])
    kkr = k * kkw[...]
    kk = kkr / jnp.maximum(jnp.sqrt(_segsum(kkr * kkr, bd[...])), 1e-12)
    k_h = k * (1.0 + (a - 1.0) * kaw[...])
    r_o[0] = r
    d_o[0] = jnp.exp(-jnp.exp(w_log))
    k_o[0] = k_h
    v_o[0] = v
    nkk_o[0] = -kk
    ka_o[0] = kk * a
    g_o[0] = g
    bonus_o[0] = _segsum(r * k_h * rk[...], bd[...]) * v

    logf = -_softplus(-(_dot(xb, w_f[...]) + b_f[...]))
    kf = _dot(xb, w_k[...])
    if not prompt:
        q_o[0] = (_dot(xb, w_q[...]) * q_scale).astype(BF16)
        kf_o[0] = kf
        vf_o[0] = _dot(xb, w_v[...])
        logf_o[0] = logf[:, 0:H_FX]
        return

    qt_o[0] = (_dot_nt(w_qt[...], xb) * q_scale).astype(BF16)
    kr_o[0] = kf.astype(BF16)
    kt_o[0] = _dot_nt(w_kt[...], xb)
    vt_o[0] = _dot_nt(w_vt[...], xb)
    logft = -_softplus(-(_dot_nt(w_ft[...], xb)[0:H_FX] + b_ft[...]))
    logft_o[0] = logft
    cum_t = _rows_times(logft, triu[...]) + carry_t[...]
    carry_t[...] = cum_t[:, tm - 1:tm]
    pieces_t = list(_split3(cum_t * LOG2E)) + [jnp.zeros((SUBLANES, tm), F32)]
    eqt = _dot(place_q[...], jnp.concatenate(pieces_t, axis=0).astype(BF16)) + ones_q[...]
    eqt_o[0] = eqt.astype(BF16)

    hi, mid, lo = _split3(logf)
    trib = tril[...]
    cum_r = (_dot(trib, hi.astype(BF16)) + _dot(trib, mid.astype(BF16)) + _dot(trib, lo.astype(BF16))
             + carry_r[...])
    carry_r[...] = cum_r[tm - 1:tm, :]
    pieces_r = jnp.concatenate(_split3(cum_r * LOG2E), axis=1).astype(BF16)
    ek_o[0] = (_dot(pieces_r, place_k[...]) + ones_k[...]).astype(BF16)


def _bias_placement():
    pairs = H_FX // 2
    place_q = np.zeros((pairs * EXTRA_ROWS, (PIECES + 1) * SUBLANES), np.float32)
    ones_q = np.zeros((pairs * EXTRA_ROWS, 1), np.float32)
    place_k = np.zeros((PIECES * LANES, pairs * LANES), np.float32)
    ones_k = np.zeros((1, pairs * LANES), np.float32)
    for h in range(H_FX):
        p, o = h // 2, (h % 2) * 2 * PIECES
        for piece in range(PIECES):
            place_q[p * EXTRA_ROWS + o + piece, piece * SUBLANES + h] = 1.0
            ones_q[p * EXTRA_ROWS + o + PIECES + piece, 0] = 1.0
            ones_k[0, p * LANES + o + piece] = 1.0
            place_k[piece * LANES + h, p * LANES + o + PIECES + piece] = -1.0
    return (jnp.asarray(place_q, BF16), jnp.asarray(ones_q), jnp.asarray(place_k, BF16), jnp.asarray(ones_k))


def _proj(x, hist0, wts, *, tm, sh, prompt):
    nb, n, _ = x.shape
    hist = hist0.shape[1]
    consts = [wts[nm] for nm in _PROJ_COMMON + (_PROJ_PROMPT if prompt else _PROJ_SAMPLE)]
    if prompt:
        tril = jnp.tril(jnp.ones((tm, tm), BF16))
        consts += [tril, tril.T]
    row = lambda c, dt=F32: (jax.ShapeDtypeStruct((nb, n, c), dt), pl.BlockSpec((1, tm, c), lambda b, t: (b, t, 0)))
    col = lambda c, dt=F32: (jax.ShapeDtypeStruct((nb, c, n), dt), pl.BlockSpec((1, c, tm), lambda b, t: (b, 0, t)))
    outs = [row(D_RW) for _ in range(8)]
    scratch = [pltpu.VMEM((hist + tm, C_RW), F32)]
    if prompt:
        outs += [col(D_FX, BF16), col(H_FX // 2 * EXTRA_ROWS, BF16), row(D_FX, BF16), row(D_FX, BF16),
                 col(D_FX), col(D_FX), col(H_FX)]
        scratch += [pltpu.VMEM((1, LANES), F32), pltpu.VMEM((H_FX, 1), F32)]
    else:
        outs += [row(D_FX, BF16), row(D_FX), row(D_FX), row(H_FX)]
    outs.append((jax.ShapeDtypeStruct((nb, hist, C_RW), F32), pl.BlockSpec((1, hist, C_RW), lambda b, t: (b, 0, 0))))
    return pl.pallas_call(
        functools.partial(_proj_kernel, tm=tm, sh=sh, hist=hist, prompt=prompt,
                          q_scale=SCALE * LOG2E if prompt else SCALE),
        grid=(nb, n // tm),
        in_specs=[pl.BlockSpec((1, tm, D_MODEL), lambda b, t: (b, t, 0)),
                  pl.BlockSpec((1, hist, C_RW), lambda b, t: (b, 0, 0))]
                 + [_const_spec(c.shape) for c in consts],
        out_specs=[o[1] for o in outs],
        out_shape=[o[0] for o in outs],
        scratch_shapes=scratch,
        compiler_params=_params(("parallel", "arbitrary")),
        name="proj",
    )(x, hist0, *consts)


SCAN_J_BLOCK = 32


def _scan_kernel(r_ref, d_ref, k_ref, nkk_ref, ka_ref, v_ref, s0_ref, out_ref, s_ref, *, tt):
    @pl.when(pl.program_id(1) == 0)
    def _():
        s_ref[...] = s0_ref[...]

    zero = jnp.zeros((HEAD_DIM, LANES), F32)

    def step(i, carry):
        row = lambda ref, j: ref[i, pl.ds(j, 1), :]

        def contract(jb, acc):
            a0, a1 = acc
            for u in range(0, SCAN_J_BLOCK, 2):
                j = jb * SCAN_J_BLOCK + u
                a0 = a0 + s_ref[j] * row(nkk_ref, j)
                a1 = a1 + s_ref[j + 1] * row(nkk_ref, j + 1)
            return a0, a1

        sa0, sa1 = lax.fori_loop(0, HEAD_DIM // SCAN_J_BLOCK, contract, (zero, zero))
        sa = sa0 + sa1
        v_t = v_ref[i]

        def update(jb, acc):
            o0, o1 = acc
            for u in range(SCAN_J_BLOCK):
                j = jb * SCAN_J_BLOCK + u
                sj = s_ref[j] * row(d_ref, j) + sa * row(ka_ref, j) + v_t * row(k_ref, j)
                s_ref[j] = sj
                if u % 2 == 0:
                    o0 = o0 + sj * row(r_ref, j)
                else:
                    o1 = o1 + sj * row(r_ref, j)
            return o0, o1

        o0, o1 = lax.fori_loop(0, HEAD_DIM // SCAN_J_BLOCK, update, (zero, zero))
        out_ref[i] = o0 + o1
        return carry

    lax.fori_loop(0, tt, step, 0)


def _scan(xs, s0, *, tt):
    steps, groups = xs[0].shape[:2]
    seq = pl.BlockSpec((tt, None, HEAD_DIM, LANES), lambda g, t: (t, g, 0, 0))
    st = pl.BlockSpec((None, HEAD_DIM, HEAD_DIM, LANES), lambda g, t: (g, 0, 0, 0))
    return pl.pallas_call(
        functools.partial(_scan_kernel, tt=tt),
        grid=(groups, steps // tt),
        in_specs=[seq] * 6 + [st],
        out_specs=[seq, st],
        out_shape=[jax.ShapeDtypeStruct(xs[0].shape, F32), jax.ShapeDtypeStruct(s0.shape, F32)],
        compiler_params=_params(("parallel", "arbitrary")),
        name="scan",
    )(*xs, s0)


def _attn_kernel(qi_ref, ki_ref, qt_ref, eqt_ref, kr_ref, ek_ref, vt_ref, g_ref, o_ref,
                 qa0, qa1, m0_s, m1_s, l0_s, l1_s, acc, *, tq, tk):
    step = pl.program_id(2)
    qi = qi_ref[step]
    ki = ki_ref[step]
    sub = lax.broadcasted_iota(jnp.int32, (LANES, 1), 0)
    head0 = sub < HEAD_DIM

    @pl.when(ki == 0)
    def _():
        q2 = qt_ref[0]
        eq = eqt_ref[0]
        zero = jnp.zeros_like(q2)
        ezero = jnp.zeros_like(eq)
        first = lax.broadcasted_iota(jnp.int32, (EXTRA_ROWS, 1), 0) < 2 * PIECES
        pad = jnp.zeros((LANES - EXTRA_ROWS, tq), BF16)
        qa0[...] = jnp.concatenate([jnp.where(head0, q2, zero), jnp.where(first, eq, ezero), pad], axis=0)
        qa1[...] = jnp.concatenate([jnp.where(head0, zero, q2), jnp.where(first, ezero, eq), pad], axis=0)
        m0_s[...] = jnp.full_like(m0_s, NEG_BIG)
        m1_s[...] = jnp.full_like(m1_s, NEG_BIG)
        l0_s[...] = jnp.zeros_like(l0_s)
        l1_s[...] = jnp.zeros_like(l1_s)
        acc[...] = jnp.zeros_like(acc)

    def update(masked):
        kaug = jnp.concatenate([kr_ref[0], ek_ref[0]], axis=1)
        vt2 = vt_ref[0]
        vz = jnp.zeros_like(vt2)
        pad = jnp.zeros((LANES - EXTRA_ROWS, tk), BF16)
        ones = jnp.where(lax.broadcasted_iota(jnp.int32, (EXTRA_ROWS, tk), 0) == 0, 1.0, 0.0).astype(BF16)
        vaug = [jnp.concatenate([jnp.where(head0, vt2, vz).astype(BF16), ones, pad], axis=0),
                jnp.concatenate([jnp.where(head0, vz, vt2).astype(BF16), ones, pad], axis=0)]
        half = tq // 2
        key_max = lambda s: jnp.max(s, axis=0, keepdims=True)
        corr = []
        outs = []
        if masked:
            scores = []
            for qa in (qa0, qa1):
                s_l = _dot(kaug[0:half], qa[:, 0:half])
                s_r = _dot(kaug, qa[:, half:tq])
                s_l = jnp.where(lax.broadcasted_iota(jnp.int32, s_l.shape, 0)
                                <= lax.broadcasted_iota(jnp.int32, s_l.shape, 1), s_l, NEG_BIG)
                s_r = jnp.where(lax.broadcasted_iota(jnp.int32, s_r.shape, 0) - half
                                <= lax.broadcasted_iota(jnp.int32, s_r.shape, 1), s_r, NEG_BIG)
                scores.append((s_l, s_r))
        else:
            scores = [_dot(kaug, qa0[...]), _dot(kaug, qa1[...])]
        for s, va, m_s in zip(scores, vaug, (m0_s, m1_s)):
            m_old = m_s[...]
            if masked:
                s_l, s_r = s
                m_new = jnp.maximum(m_old, jnp.concatenate([key_max(s_l), key_max(s_r)], axis=1))
                p_l = jnp.exp2(s_l - m_new[:, 0:half]).astype(BF16)
                p_r = jnp.exp2(s_r - m_new[:, half:tq]).astype(BF16)
                out = jnp.concatenate([_dot(va[:, 0:half], p_l), _dot(va, p_r)], axis=1)
            else:
                m_new = jnp.maximum(m_old, key_max(s))
                out = _dot(va, jnp.exp2(s - m_new).astype(BF16))
            m_s[...] = m_new
            corr.append(jnp.exp2(m_old - m_new))
            outs.append(out)
        l0_s[...] = l0_s[...] * corr[0] + outs[0][LANES:LANES + SUBLANES]
        l1_s[...] = l1_s[...] * corr[1] + outs[1][LANES:LANES + SUBLANES]
        acc[...] = acc[...] * jnp.where(head0, corr[0], corr[1]) + outs[0][:LANES] + outs[1][:LANES]

    @pl.when(ki < qi)
    def _():
        update(False)

    @pl.when(ki == qi)
    def _():
        update(True)
        ot = acc[...] * jnp.where(head0, 1.0 / l0_s[0:1, :], 1.0 / l1_s[0:1, :])
        o = ot.T
        o2 = o * o
        lane0 = lax.broadcasted_iota(jnp.int32, (1, LANES), 1) < HEAD_DIM
        ms0 = jnp.sum(jnp.where(lane0, o2, 0.0), -1, keepdims=True) * (1.0 / HEAD_DIM)
        ms1 = jnp.sum(jnp.where(lane0, 0.0, o2), -1, keepdims=True) * (1.0 / HEAD_DIM)
        rn = jnp.where(lane0, lax.rsqrt(ms0 + RMS_EPS), lax.rsqrt(ms1 + RMS_EPS))
        o_ref[0] = (o * rn * g_ref[...]).astype(BF16)


def _attn(qt, eqt, kr, ek, vt, gain, *, tq):
    nb, _, s = qt.shape
    tk = tq
    nq = s // tq
    pairs = D_FX // LANES
    qi_tab = jnp.asarray([qi for qi in range(nq) for _ in range(qi + 1)], jnp.int32)
    ki_tab = jnp.asarray([ki for qi in range(nq) for ki in range(qi + 1)], jnp.int32)
    q_col = lambda b, p, s_, qt_, kt_: (b, p, qt_[s_])
    k_row = lambda b, p, s_, qt_, kt_: (b, kt_[s_], p)
    return pl.pallas_call(
        functools.partial(_attn_kernel, tq=tq, tk=tk),
        grid_spec=pltpu.PrefetchScalarGridSpec(
            num_scalar_prefetch=2,
            grid=(nb, pairs, int(qi_tab.shape[0])),
            in_specs=[pl.BlockSpec((1, LANES, tq), q_col),
                      pl.BlockSpec((1, EXTRA_ROWS, tq), q_col),
                      pl.BlockSpec((1, tk, LANES), k_row),
                      pl.BlockSpec((1, tk, LANES), k_row),
                      pl.BlockSpec((1, LANES, tk), lambda b, p, s_, qt_, kt_: (b, p, kt_[s_])),
                      pl.BlockSpec((1, LANES), lambda b, p, s_, qt_, kt_: (0, p))],
            out_specs=pl.BlockSpec((1, tq, LANES), lambda b, p, s_, qt_, kt_: (b, qt_[s_], p)),
            scratch_shapes=[pltpu.VMEM((2 * LANES, tq), BF16), pltpu.VMEM((2 * LANES, tq), BF16),
                            pltpu.VMEM((1, tq), F32), pltpu.VMEM((1, tq), F32),
                            pltpu.VMEM((SUBLANES, tq), F32), pltpu.VMEM((SUBLANES, tq), F32),
                            pltpu.VMEM((LANES, tq), F32)]),
        out_shape=jax.ShapeDtypeStruct((nb, s, D_FX), BF16),
        compiler_params=_params(("parallel", "parallel", "arbitrary")),
        name="attn",
    )(qi_tab, ki_tab, qt, eqt, kr, ek, vt, gain)


def _dec_attn_kernel(pt_ref, q_ref, kn_ref, vn_ref, lfn_ref, gt_ref, hm_ref, g_ref, *rest, pc, nc, t_new):
    kp = rest[0:pc]
    vp = rest[pc:2 * pc]
    lp = rest[2 * pc:3 * pc]
    o_ref, m_s, l_s, acc, carry = rest[3 * pc:]
    c = pl.program_id(1)
    q = q_ref[0]
    per_query = lambda bias: jnp.concatenate([bias] * t_new, axis=0)

    def update(s, pv_of):
        m_old = m_s[...]
        m_new = jnp.maximum(m_old, jnp.max(s, -1, keepdims=True))
        corr = jnp.exp(m_old - m_new)
        p = jnp.exp(s - m_new)
        l_s[...] = l_s[...] * corr + jnp.sum(p, -1, keepdims=True)
        m_s[...] = m_new
        acc[...] = acc[...] * corr + pv_of(p.astype(BF16))

    @pl.when(c == 0)
    def _():
        m_s[...] = jnp.full_like(m_s, NEG_BIG)
        l_s[...] = jnp.zeros_like(l_s)
        acc[...] = jnp.zeros_like(acc)
        lf = lfn_ref[0]
        nk = kn_ref.shape[1]
        s = _dot_nt(q, kn_ref[0].astype(BF16)) + per_query(_rows_times(lf, gt_ref[...])[:, 0:nk])
        key_t = lax.broadcasted_iota(jnp.int32, s.shape, 1)
        qry_t = lax.broadcasted_iota(jnp.int32, s.shape, 0) // H_FX
        update(jnp.where(key_t <= qry_t, s, NEG_BIG), lambda p: _dot(p, vn_ref[0].astype(BF16)))
        carry[...] = jnp.sum(lf, -1, keepdims=True)

    lf_all = jnp.concatenate([lp[i][...] for i in range(pc)], axis=0)
    within = _rows_times(lf_all, gt_ref[...])
    totals = jnp.sum(lf_all, -1, keepdims=True)
    later = carry[...]
    scores = []
    for i in range(pc):
        rows = slice(i * H_FX, (i + 1) * H_FX)
        scores.append(_dot(q, kp[i][...].astype(BF16)) + per_query(within[rows] + later))
        later = later + totals[rows]
    carry[...] = later

    def pages_pv(p):
        out = _dot_nt(p[:, 0:PAGE_SIZE], vp[0][...].astype(BF16))
        for i in range(1, pc):
            out = out + _dot_nt(p[:, i * PAGE_SIZE:(i + 1) * PAGE_SIZE], vp[i][...].astype(BF16))
        return out

    update(jnp.concatenate(scores, axis=1), pages_pv)

    @pl.when(c == nc - 1)
    def _():
        o = acc[...] * (1.0 / l_s[...]) * hm_ref[...]
        ms = jnp.sum(o * o, -1, keepdims=True) * (1.0 / HEAD_DIM)
        o = o * lax.rsqrt(ms + RMS_EPS)
        per_t = [jnp.sum(o[t * H_FX:(t + 1) * H_FX], axis=0, keepdims=True) for t in range(t_new)]
        o_ref[0] = (jnp.concatenate(per_t, axis=0) * g_ref[...]).astype(BF16)


def _dec_attn(page_table, qbig, k_new, v_new, lf_new, cache_kt, cache_vt, cache_lft, head_mask, gain, *, pc, t_new):
    nb, n_pages = page_table.shape
    nc = n_pages // pc
    nk = k_new.shape[1]
    gt = jnp.triu(jnp.ones((PAGE_SIZE, PAGE_SIZE), BF16), 1).T
    rows = t_new * H_FX

    def page_spec(shape, i):
        return pl.BlockSpec((None,) + shape, lambda b, c, pt: (pt[b, n_pages - 1 - (c * pc + i)], 0, 0))

    per_b = lambda shape: pl.BlockSpec((1,) + shape, lambda b, c, pt: (b, 0, 0))
    const = lambda shape: pl.BlockSpec(shape, lambda b, c, pt: (0, 0))
    in_specs = ([per_b((rows, D_FX)), per_b((nk, D_FX)), per_b((nk, D_FX)), per_b((H_FX, LANES)),
                 const((PAGE_SIZE, PAGE_SIZE)), const((rows, D_FX)), const((1, D_FX))]
                + [page_spec((D_FX, PAGE_SIZE), i) for i in range(pc)]
                + [page_spec((D_FX, PAGE_SIZE), i) for i in range(pc)]
                + [page_spec((H_FX, PAGE_SIZE), i) for i in range(pc)])
    return pl.pallas_call(
        functools.partial(_dec_attn_kernel, pc=pc, nc=nc, t_new=t_new),
        grid_spec=pltpu.PrefetchScalarGridSpec(
            num_scalar_prefetch=1,
            grid=(nb, nc),
            in_specs=in_specs,
            out_specs=pl.BlockSpec((1, t_new, D_FX), lambda b, c, pt: (b, 0, 0)),
            scratch_shapes=[pltpu.VMEM((rows, 1), F32), pltpu.VMEM((rows, 1), F32),
                            pltpu.VMEM((rows, D_FX), F32), pltpu.VMEM((H_FX, 1), F32)]),
        out_shape=jax.ShapeDtypeStruct((nb, t_new, D_FX), BF16),
        compiler_params=_params(("parallel", "arbitrary")),
        name="dec_attn",
    )(page_table, qbig, k_new, v_new, lf_new, gt, head_mask, gain,
      *([cache_kt] * pc), *([cache_vt] * pc), *([cache_lft] * pc))


FF_CHUNK = D_FF // 2
_TAIL_CONSTS = ['ln_in_g', 'ln_in_b', 'rw_gn_g', 'rw_gn_b', 'bd', 'wo_rw', 'wo_fx', 'ln1_g', 'ln1_b',
                'w_up_a', 'w_up_b', 'conv_w', 'conv_b', 'w_down', 'ln2_g', 'ln2_b', 'w_pg', 'b_pg', 'w_ple']


def _tail_kernel(x_ref, wkv_ref, bonus_ref, g_ref, fx_ref, pe_ref, hist0_ref,
                 lng, lnb, gng, gnb, bd, wo_rw, wo_fx, l1g, l1b,
                 wa, wb, cw, cb, wd, l2g, l2b, wpg, bpg, wple, y_ref, hist_o, aext, *, tm, sh, hist):
    @pl.when(pl.program_id(1) == 0)
    def _():
        aext[0:hist, :] = hist0_ref[0]

    xn = _layer_norm(x_ref[0], lng[...], lnb[...])
    wkv = wkv_ref[0]
    mean = _segsum(wkv, bd[...]) * (1.0 / HEAD_DIM)
    wc = wkv - mean
    var = _segsum(wc * wc, bd[...]) * (1.0 / HEAD_DIM)
    wn = wc * lax.rsqrt(var + GN_EPS) * gng[...] + gnb[...]
    rw_y = (wn + bonus_ref[0]) * g_ref[0]
    mix = _dot(rw_y.astype(BF16), wo_rw[...]) + _dot(fx_ref[0], wo_fx[...])
    x1 = _layer_norm(ALPHA * xn + mix, l1g[...], l1b[...])
    xb = x1.astype(BF16)
    y = jnp.zeros((tm, D_MODEL), F32)
    for c0 in range(0, D_FF, FF_CHUNK):
        cs = slice(c0, c0 + FF_CHUNK)
        a = _dot(xb, wa[:, cs])
        aext[hist:hist + tm, cs] = a
        p1 = aext[hist - sh:hist - sh + tm, cs]
        p2 = aext[hist - 2 * sh:hist - 2 * sh + tm, cs]
        conv = cb[:, cs] + p2 * cw[0:1, cs] + p1 * cw[1:2, cs] + a * cw[2:3, cs]
        gelu = 0.5 * conv * (1.0 + lax.erf(conv * (2.0 ** -0.5)))
        h = gelu * _dot(xb, wb[:, cs])
        y = y + _dot(h.astype(BF16), wd[cs, :])
    new_hist = aext[tm:tm + hist, :]
    hist_o[0] = new_hist
    aext[0:hist, :] = new_hist
    x2 = _layer_norm(ALPHA * x1 + y, l2g[...], l2b[...])
    gate = jax.nn.sigmoid(_dot(x2.astype(BF16), wpg[...]) + bpg[...])
    y_ref[0] = x2 + gate * _dot(pe_ref[0].astype(BF16), wple[...])


def _tail(x, wkv, bonus, g, fx, pe, hist0, wts, *, tm, sh):
    nb, n, _ = x.shape
    hist = hist0.shape[1]
    consts = [wts[nm] for nm in _TAIL_CONSTS]
    row = lambda c: pl.BlockSpec((1, tm, c), lambda b, t: (b, t, 0))
    hist_spec = pl.BlockSpec((1, hist, D_FF), lambda b, t: (b, 0, 0))
    return pl.pallas_call(
        functools.partial(_tail_kernel, tm=tm, sh=sh, hist=hist),
        grid=(nb, n // tm),
        in_specs=[row(D_MODEL), row(D_RW), row(D_RW), row(D_RW), row(D_FX), row(PLE_DIM), hist_spec]
                 + [_const_spec(c.shape) for c in consts],
        out_specs=[row(D_MODEL), hist_spec],
        out_shape=[jax.ShapeDtypeStruct((nb, n, D_MODEL), F32), jax.ShapeDtypeStruct((nb, hist, D_FF), F32)],
        scratch_shapes=[pltpu.VMEM((hist + tm, D_FF), F32)],
        compiler_params=_params(("parallel", "arbitrary")),
        name="tail",
    )(x, wkv, bonus, g, fx, pe, hist0, *consts)


def _prep_weights(ln_in_g, ln_in_b, w_in, b_f, rw_mu, rw_w0, rw_w2, rw_a0, rw_a2, rw_g2, rw_kk, rw_ka, rw_rk,
                  rw_gn_g, rw_gn_b, fx_norm_g, w_o, ln1_g, ln1_b, w_up, conv_w, conv_b, w_down, ln2_g, ln2_b,
                  w_ple, w_pg, b_pg):
    r2 = lambda a: a.reshape(1, -1)
    bf = lambda a: a.astype(BF16)
    lora_rows = R_DECAY + R_ICL
    seg = jnp.arange(D_RW) // HEAD_DIM
    fx0 = C_RW
    w_f = w_in[:, fx0 + 3 * D_FX:]
    place_q, ones_q, place_k, ones_k = _bias_placement()
    return dict(
        ln_in_g=r2(ln_in_g), ln_in_b=r2(ln_in_b),
        w_rw=bf(w_in[:, :C_RW]),
        w_q=bf(w_in[:, fx0:fx0 + D_FX]),
        w_k=bf(w_in[:, fx0 + D_FX:fx0 + 2 * D_FX]), w_v=bf(w_in[:, fx0 + 2 * D_FX:fx0 + 3 * D_FX]),
        w_qt=bf(w_in[:, fx0:fx0 + D_FX].T), w_kt=bf(w_in[:, fx0 + D_FX:fx0 + 2 * D_FX].T), w_vt=bf(w_in[:, fx0 + 2 * D_FX:fx0 + 3 * D_FX].T),
        w_f=bf(jnp.pad(w_f, ((0, 0), (0, LANES - H_FX)))),
        b_f=jnp.pad(b_f, (0, LANES - H_FX)).reshape(1, LANES),
        w_ft=bf(jnp.pad(w_f.T, ((0, PACKED_ROWS - H_FX), (0, 0)))), b_ft=b_f.reshape(H_FX, 1),
        place_q=place_q, ones_q=ones_q, place_k=place_k, ones_k=ones_k,
        rw_mu=r2(rw_mu), rw_w0=r2(rw_w0), rw_a0=r2(rw_a0),
        rw_w2=bf(jnp.pad(rw_w2, ((0, lora_rows - R_DECAY), (0, 0)))),
        rw_a2=bf(jnp.pad(rw_a2, ((lora_rows - R_ICL, 0), (0, 0)))),
        rw_g2=bf(rw_g2), rw_kk=r2(rw_kk), rw_ka=r2(rw_ka), rw_rk=r2(rw_rk),
        bd=(seg[:, None] == seg[None, :]).astype(BF16),
        rw_gn_g=r2(rw_gn_g), rw_gn_b=r2(rw_gn_b), fx_norm_g=r2(fx_norm_g),
        wo_rw=bf(w_o[:D_RW]), wo_fx=bf(w_o[D_RW:]), ln1_g=r2(ln1_g), ln1_b=r2(ln1_b),
        w_up_a=bf(w_up[:, :D_FF]), w_up_b=bf(w_up[:, D_FF:]), conv_w=conv_w, conv_b=r2(conv_b),
        w_down=bf(w_down), ln2_g=r2(ln2_g), ln2_b=r2(ln2_b), w_ple=bf(w_ple), w_pg=bf(w_pg), b_pg=r2(b_pg))


def _prompt_layer(x, pe, wts):
    nb, s, _ = x.shape
    hist = SUBLANES
    (r, d, k, v, nkk, ka, g, bonus, qt, eqt, kr, ek, kt, vt, logft, shift_hist) = _proj(
        x, jnp.zeros((nb, hist, C_RW), F32), wts, tm=512, sh=1, prompt=True)
    to_scan = lambda a: a.reshape(nb, s, H_RW, HEAD_DIM).transpose(1, 3, 0, 2).reshape(s, 1, HEAD_DIM, nb * H_RW)
    wkv, s_fin = _scan([to_scan(a) for a in (r, d, k, nkk, ka, v)],
                       jnp.zeros((1, HEAD_DIM, HEAD_DIM, nb * H_RW), F32), tt=16)
    wkv = wkv.reshape(s, HEAD_DIM, nb, H_RW).transpose(2, 0, 3, 1).reshape(nb, s, D_RW)
    s_fin = s_fin.reshape(HEAD_DIM, HEAD_DIM, nb, H_RW).transpose(2, 3, 1, 0)
    fx = _attn(qt, eqt, kr, ek, vt, wts['fx_norm_g'], tq=1024)
    y, conv_hist = _tail(x, wkv, bonus, g, fx, pe, jnp.zeros((nb, hist, D_FF), F32), wts, tm=512, sh=1)
    heads = lambda a: a.reshape(nb, H_FX, HEAD_DIM, s).transpose(0, 3, 1, 2)
    return y, (heads(kt), heads(vt), logft.transpose(0, 2, 1), s_fin, shift_hist[:, hist - 1],
               conv_hist[:, hist - (CONV_W - 1):])


def _sample_layer(x, pe, state_shift, state_wkv, state_conv, cache_k, cache_v, cache_logf, page_table, wts):
    nb, t_new, _ = x.shape
    n = nb * t_new
    tmajor = lambda a: a.transpose(1, 0, 2).reshape(1, n, a.shape[-1])
    bmajor = lambda a: a.reshape(t_new, nb, a.shape[-1]).transpose(1, 0, 2)
    x_tm = tmajor(x)
    (r, d, k, v, nkk, ka, g, bonus, q, kf, vf, logf, shift_hist) = _proj(
        x_tm, state_shift.reshape(1, nb, C_RW), wts, tm=nb, sh=nb, prompt=False)
    to_scan = lambda a: a.reshape(t_new, nb, H_RW, HEAD_DIM).transpose(0, 2, 3, 1)
    wkv, s_fin = _scan([to_scan(a) for a in (r, d, k, nkk, ka, v)], state_wkv.transpose(1, 3, 2, 0), tt=t_new)
    wkv = wkv.transpose(0, 3, 1, 2).reshape(1, n, D_RW)
    s_fin = s_fin.transpose(3, 0, 2, 1)

    kf_b, vf_b, logf_b = bmajor(kf), bmajor(vf), bmajor(logf)
    head_mask = (jnp.arange(D_FX)[None, :] // HEAD_DIM == jnp.arange(H_FX)[:, None]).astype(BF16)
    qbig = (bmajor(q)[:, :, None, :] * head_mask).reshape(nb, t_new * H_FX, D_FX)
    nk = PACKED_ROWS
    pad_rows = lambda a: jnp.pad(a, ((0, 0), (0, nk - t_new), (0, 0)))
    lf_new = jnp.pad(logf_b.transpose(0, 2, 1), ((0, 0), (0, 0), (0, LANES - t_new)))
    n_pool = cache_k.shape[0]
    pages_t = lambda c: c.transpose(0, 2, 3, 1).reshape(n_pool, D_FX, PAGE_SIZE)
    fx = _dec_attn(page_table, qbig, pad_rows(kf_b), pad_rows(vf_b), lf_new,
                   pages_t(cache_k), pages_t(cache_v), cache_logf.transpose(0, 2, 1),
                   jnp.tile(head_mask.astype(F32), (t_new, 1)), wts['fx_norm_g'], pc=32, t_new=t_new)
    conv0 = state_conv.transpose(1, 0, 2).reshape(1, (CONV_W - 1) * nb, D_FF)
    y, conv_hist = _tail(x_tm, wkv, bonus, g, tmajor(fx), tmajor(pe), conv0, wts, tm=nb, sh=nb)
    heads = lambda a: a.reshape(nb, t_new, H_FX, HEAD_DIM)
    conv_new = conv_hist.reshape(CONV_W - 1, nb, D_FF).transpose(1, 0, 2)
    return bmajor(y[0]), (heads(kf_b), heads(vf_b), logf_b, s_fin, shift_hist[0], conv_new)


def kernel(x_prompt, x_sample, cache_k, cache_v, cache_logf, state_wkv, state_shift, state_conv, page_table, p_prompt, p_sample, ln_in_g, ln_in_b, w_in, b_f, rw_mu, rw_w0, rw_w2, rw_a0, rw_a2, rw_g2, rw_kk, rw_ka, rw_rk, rw_gn_g, rw_gn_b, fx_norm_g, w_o, ln1_g, ln1_b, w_up, conv_w, conv_b, w_down, ln2_g, ln2_b, w_ple, w_pg, b_pg):
    assert w_in.shape[0] == 1, "single-layer model"
    wts = _prep_weights(ln_in_g, ln_in_b, w_in[0], b_f[0], rw_mu[0], rw_w0[0], rw_w2[0], rw_a0[0], rw_a2[0],
                        rw_g2[0], rw_kk[0], rw_ka[0], rw_rk[0].reshape(-1), rw_gn_g[0], rw_gn_b[0], fx_norm_g[0],
                        w_o[0], ln1_g[0], ln1_b[0], w_up[0], conv_w[0], conv_b[0], w_down[0], ln2_g[0], ln2_b[0],
                        w_ple[0], w_pg[0], b_pg[0])
    yp, st_p = _prompt_layer(x_prompt, p_prompt[0], wts)
    ys, st_s = _sample_layer(x_sample, p_sample[0], state_shift[0], state_wkv[0], state_conv[0],
                             cache_k[0], cache_v[0], cache_logf[0], page_table, wts)
    return (yp, ys) + tuple(a[None] for a in st_p) + tuple(a[None] for a in st_s)
```

```python
import functools
import math

import jax
import jax.numpy as jnp
import numpy as np
from jax import lax
from jax.experimental import pallas as pl
from jax.experimental.pallas import tpu as pltpu

D_MODEL = 1024
HEAD_DIM = 64
D_RW = 512
H_RW = 8
D_FX = 512
H_FX = 8
R_DECAY = 64
R_ICL = 64
R_GATE = 128
C_RW = 3 * D_RW + R_DECAY + R_ICL + R_GATE
D_FF = 2816
CONV_W = 3
PLE_DIM = 256
PAGE_SIZE = 128
LN_EPS = 1e-5
GN_EPS = 64e-5
RMS_EPS = 1e-6
SCALE = HEAD_DIM ** -0.5
ALPHA = 2.0 ** 0.25
LOG2E = math.log2(math.e)

LANES = 128
SUBLANES = 8
PACKED_ROWS = 16
NEG_BIG = -1e30
VMEM_LIMIT = 56 * 1024 * 1024
EXTRA_ROWS = PACKED_ROWS
PIECES = 3

BF16 = jnp.bfloat16
F32 = jnp.float32


def _dot(a, b):
    return jnp.dot(a, b, preferred_element_type=F32)


def _dot_nt(a, b):
    return lax.dot_general(a, b, (((1,), (1,)), ((), ())), preferred_element_type=F32)


def _split3(x):
    hi = x.astype(BF16).astype(F32)
    r1 = x - hi
    mid = r1.astype(BF16).astype(F32)
    lo = (r1 - mid).astype(BF16).astype(F32)
    return hi, mid, lo


def _segsum(x, bd):
    return _dot(x.astype(BF16), bd)


def _rows_times(x, mat):
    n = x.shape[0]
    pieces = list(_split3(x))
    pad = (-PIECES * n) % PACKED_ROWS
    if pad:
        pieces.append(jnp.zeros((pad, x.shape[1]), F32))
    r = _dot(jnp.concatenate(pieces, axis=0).astype(BF16), mat)
    return r[0:n] + r[n:2 * n] + r[2 * n:3 * n]


def _layer_norm(x, g, b):
    mu = jnp.mean(x, -1, keepdims=True)
    xc = x - mu
    var = jnp.mean(xc * xc, -1, keepdims=True)
    return xc * lax.rsqrt(var + LN_EPS) * g + b


def _softplus(y):
    return jnp.maximum(y, 0.0) + jnp.log1p(jnp.exp(-jnp.abs(y)))


def _const_spec(shape):
    nd = len(shape)
    return pl.BlockSpec(shape, lambda *_: (0,) * nd, pipeline_mode=pl.Buffered(1))


def _params(sem):
    return pltpu.CompilerParams(dimension_semantics=sem, vmem_limit_bytes=VMEM_LIMIT)


_PROJ_COMMON = ['ln_in_g', 'ln_in_b', 'w_rw', 'rw_mu', 'rw_w0', 'rw_w2', 'rw_a0', 'rw_a2', 'rw_g2', 'rw_kk',
                'rw_ka', 'rw_rk', 'bd', 'w_k', 'w_f', 'b_f']
_PROJ_PROMPT = ['w_qt', 'w_kt', 'w_vt', 'w_ft', 'b_ft', 'place_q', 'ones_q', 'place_k', 'ones_k']
_PROJ_SAMPLE = ['w_q', 'w_v']


def _proj_kernel(*refs, tm, sh, hist, prompt, q_scale):
    nc = len(_PROJ_COMMON)
    x_ref, hist0_ref = refs[:2]
    (lng, lnb, w_rw, mu, w0, w2, a0, a2, g2, kkw, kaw, rk, bd, w_k, w_f, b_f) = refs[2:2 + nc]
    pos = 2 + nc
    if prompt:
        (w_qt, w_kt, w_vt, w_ft, b_ft, place_q, ones_q, place_k, ones_k, tril, triu) = refs[pos:pos + 11]
        pos += 11
    else:
        w_q, w_v = refs[pos:pos + 2]
        pos += 2
    (r_o, d_o, k_o, v_o, nkk_o, ka_o, g_o, bonus_o) = refs[pos:pos + 8]
    pos += 8
    if prompt:
        (qt_o, eqt_o, kr_o, ek_o, kt_o, vt_o, logft_o, shift_o, zext, carry_r, carry_t) = refs[pos:]
    else:
        (q_o, kf_o, vf_o, logf_o, shift_o, zext) = refs[pos:]

    t = pl.program_id(1)

    @pl.when(t == 0)
    def _():
        zext[0:hist, :] = hist0_ref[0]
        if prompt:
            carry_r[...] = jnp.zeros_like(carry_r)
            carry_t[...] = jnp.zeros_like(carry_t)

    xn = _layer_norm(x_ref[0], lng[...], lnb[...])
    xb = xn.astype(BF16)

    zr = _dot(xb, w_rw[...])
    zext[hist:hist + tm, :] = zr
    prev = zext[hist - sh:hist - sh + tm, :]
    zs = zr + (prev - zr) * mu[...]
    new_hist = zext[tm:tm + hist, :]
    shift_o[0] = new_hist
    zext[0:hist, :] = new_hist

    r = zs[:, 0:D_RW]
    k = zs[:, D_RW:2 * D_RW]
    v = zs[:, 2 * D_RW:3 * D_RW]
    lw = zs[:, 3 * D_RW:3 * D_RW + R_DECAY + R_ICL]
    xg = zs[:, 3 * D_RW + R_DECAY + R_ICL:C_RW]
    w_log = -_softplus(-(w0[...] + _dot(jnp.tanh(lw).astype(BF16), w2[...]))) - 0.5
    a = jax.nn.sigmoid(a0[...] + _dot(lw.astype(BF16), a2[...]))
    g = _dot(jax.nn.sigmoid(xg).astype(BF16), g2[...])
    kkr = k * kkw[...]
    kk = kkr / jnp.maximum(jnp.sqrt(_segsum(kkr * kkr, bd[...])), 1e-12)
    k_h = k * (1.0 + (a - 1.0) * kaw[...])
    r_o[0] = r
    d_o[0] = jnp.exp(-jnp.exp(w_log))
    k_o[0] = k_h
    v_o[0] = v
    nkk_o[0] = -kk
    ka_o[0] = kk * a
    g_o[0] = g
    bonus_o[0] = _segsum(r * k_h * rk[...], bd[...]) * v

    logf = -_softplus(-(_dot(xb, w_f[...]) + b_f[...]))
    kf = _dot(xb, w_k[...])
    if not prompt:
        q_o[0] = (_dot(xb, w_q[...]) * q_scale).astype(BF16)
        kf_o[0] = kf
        vf_o[0] = _dot(xb, w_v[...])
        logf_o[0] = logf[:, 0:H_FX]
        return

    qt_o[0] = (_dot_nt(w_qt[...], xb) * q_scale).astype(BF16)
    kr_o[0] = kf.astype(BF16)
    kt_o[0] = _dot_nt(w_kt[...], xb)
    vt_o[0] = _dot_nt(w_vt[...], xb)
    logft = -_softplus(-(_dot_nt(w_ft[...], xb)[0:H_FX] + b_ft[...]))
    logft_o[0] = logft
    cum_t = _rows_times(logft, triu[...]) + carry_t[...]
    carry_t[...] = cum_t[:, tm - 1:tm]
    pieces_t = list(_split3(cum_t * LOG2E)) + [jnp.zeros((SUBLANES, tm), F32)]
    eqt = _dot(place_q[...], jnp.concatenate(pieces_t, axis=0).astype(BF16)) + ones_q[...]
    eqt_o[0] = eqt.astype(BF16)

    hi, mid, lo = _split3(logf)
    trib = tril[...]
    cum_r = (_dot(trib, hi.astype(BF16)) + _dot(trib, mid.astype(BF16)) + _dot(trib, lo.astype(BF16))
             + carry_r[...])
    carry_r[...] = cum_r[tm - 1:tm, :]
    pieces_r = jnp.concatenate(_split3(cum_r * LOG2E), axis=1).astype(BF16)
    ek_o[0] = (_dot(pieces_r, place_k[...]) + ones_k[...]).astype(BF16)


def _bias_placement():
    pairs = H_FX // 2
    place_q = np.zeros((pairs * EXTRA_ROWS, (PIECES + 1) * SUBLANES), np.float32)
    ones_q = np.zeros((pairs * EXTRA_ROWS, 1), np.float32)
    place_k = np.zeros((PIECES * LANES, pairs * LANES), np.float32)
    ones_k = np.zeros((1, pairs * LANES), np.float32)
    for h in range(H_FX):
        p, o = h // 2, (h % 2) * 2 * PIECES
        for piece in range(PIECES):
            place_q[p * EXTRA_ROWS + o + piece, piece * SUBLANES + h] = 1.0
            ones_q[p * EXTRA_ROWS + o + PIECES + piece, 0] = 1.0
            ones_k[0, p * LANES + o + piece] = 1.0
            place_k[piece * LANES + h, p * LANES + o + PIECES + piece] = -1.0
    return (jnp.asarray(place_q, BF16), jnp.asarray(ones_q), jnp.asarray(place_k, BF16), jnp.asarray(ones_k))


def _proj(x, hist0, wts, *, tm, sh, prompt):
    nb, n, _ = x.shape
    hist = hist0.shape[1]
    consts = [wts[nm] for nm in _PROJ_COMMON + (_PROJ_PROMPT if prompt else _PROJ_SAMPLE)]
    if prompt:
        tril = jnp.tril(jnp.ones((tm, tm), BF16))
        consts += [tril, tril.T]
    row = lambda c, dt=F32: (jax.ShapeDtypeStruct((nb, n, c), dt), pl.BlockSpec((1, tm, c), lambda b, t: (b, t, 0)))
    col = lambda c, dt=F32: (jax.ShapeDtypeStruct((nb, c, n), dt), pl.BlockSpec((1, c, tm), lambda b, t: (b, 0, t)))
    outs = [row(D_RW) for _ in range(8)]
    scratch = [pltpu.VMEM((hist + tm, C_RW), F32)]
    if prompt:
        outs += [col(D_FX, BF16), col(H_FX // 2 * EXTRA_ROWS, BF16), row(D_FX, BF16), row(D_FX, BF16),
                 col(D_FX), col(D_FX), col(H_FX)]
        scratch += [pltpu.VMEM((1, LANES), F32), pltpu.VMEM((H_FX, 1), F32)]
    else:
        outs += [row(D_FX, BF16), row(D_FX), row(D_FX), row(H_FX)]
    outs.append((jax.ShapeDtypeStruct((nb, hist, C_RW), F32), pl.BlockSpec((1, hist, C_RW), lambda b, t: (b, 0, 0))))
    return pl.pallas_call(
        functools.partial(_proj_kernel, tm=tm, sh=sh, hist=hist, prompt=prompt,
                          q_scale=SCALE * LOG2E if prompt else SCALE),
        grid=(nb, n // tm),
        in_specs=[pl.BlockSpec((1, tm, D_MODEL), lambda b, t: (b, t, 0)),
                  pl.BlockSpec((1, hist, C_RW), lambda b, t: (b, 0, 0))]
                 + [_const_spec(c.shape) for c in consts],
        out_specs=[o[1] for o in outs],
        out_shape=[o[0] for o in outs],
        scratch_shapes=scratch,
        compiler_params=_params(("parallel", "arbitrary")),
        name="proj",
    )(x, hist0, *consts)


SCAN_J_BLOCK = 32


def _scan_kernel(r_ref, d_ref, k_ref, nkk_ref, ka_ref, v_ref, s0_ref, out_ref, s_ref, *, tt):
    @pl.when(pl.program_id(1) == 0)
    def _():
        s_ref[...] = s0_ref[...]

    zero = jnp.zeros((HEAD_DIM, LANES), F32)

    def step(i, carry):
        row = lambda ref, j: ref[i, pl.ds(j, 1), :]

        def contract(jb, acc):
            a0, a1 = acc
            for u in range(0, SCAN_J_BLOCK, 2):
                j = jb * SCAN_J_BLOCK + u
                a0 = a0 + s_ref[j] * row(nkk_ref, j)
                a1 = a1 + s_ref[j + 1] * row(nkk_ref, j + 1)
            return a0, a1

        sa0, sa1 = lax.fori_loop(0, HEAD_DIM // SCAN_J_BLOCK, contract, (zero, zero))
        sa = sa0 + sa1
        v_t = v_ref[i]

        def update(jb, acc):
            o0, o1 = acc
            for u in range(SCAN_J_BLOCK):
                j = jb * SCAN_J_BLOCK + u
                sj = s_ref[j] * row(d_ref, j) + sa * row(ka_ref, j) + v_t * row(k_ref, j)
                s_ref[j] = sj
                if u % 2 == 0:
                    o0 = o0 + sj * row(r_ref, j)
                else:
                    o1 = o1 + sj * row(r_ref, j)
            return o0, o1

        o0, o1 = lax.fori_loop(0, HEAD_DIM // SCAN_J_BLOCK, update, (zero, zero))
        out_ref[i] = o0 + o1
        return carry

    lax.fori_loop(0, tt, step, 0)


def _scan(xs, s0, *, tt):
    steps, groups = xs[0].shape[:2]
    seq = pl.BlockSpec((tt, None, HEAD_DIM, LANES), lambda g, t: (t, g, 0, 0))
    st = pl.BlockSpec((None, HEAD_DIM, HEAD_DIM, LANES), lambda g, t: (g, 0, 0, 0))
    return pl.pallas_call(
        functools.partial(_scan_kernel, tt=tt),
        grid=(groups, steps // tt),
        in_specs=[seq] * 6 + [st],
        out_specs=[seq, st],
        out_shape=[jax.ShapeDtypeStruct(xs[0].shape, F32), jax.ShapeDtypeStruct(s0.shape, F32)],
        compiler_params=_params(("parallel", "arbitrary")),
        name="scan",
    )(*xs, s0)


def _attn_kernel(qi_ref, ki_ref, qt_ref, eqt_ref, kr_ref, ek_ref, vt_ref, g_ref, o_ref,
                 qa0, qa1, m0_s, m1_s, l0_s, l1_s, acc, *, tq, tk):
    step = pl.program_id(2)
    qi = qi_ref[step]
    ki = ki_ref[step]
    sub = lax.broadcasted_iota(jnp.int32, (LANES, 1), 0)
    head0 = sub < HEAD_DIM

    @pl.when(ki == 0)
    def _():
        q2 = qt_ref[0]
        eq = eqt_ref[0]
        zero = jnp.zeros_like(q2)
        ezero = jnp.zeros_like(eq)
        first = lax.broadcasted_iota(jnp.int32, (EXTRA_ROWS, 1), 0) < 2 * PIECES
        pad = jnp.zeros((LANES - EXTRA_ROWS, tq), BF16)
        qa0[...] = jnp.concatenate([jnp.where(head0, q2, zero), jnp.where(first, eq, ezero), pad], axis=0)
        qa1[...] = jnp.concatenate([jnp.where(head0, zero, q2), jnp.where(first, ezero, eq), pad], axis=0)
        m0_s[...] = jnp.full_like(m0_s, NEG_BIG)
        m1_s[...] = jnp.full_like(m1_s, NEG_BIG)
        l0_s[...] = jnp.zeros_like(l0_s)
        l1_s[...] = jnp.zeros_like(l1_s)
        acc[...] = jnp.zeros_like(acc)

    def update(masked):
        kaug = jnp.concatenate([kr_ref[0], ek_ref[0]], axis=1)
        vt2 = vt_ref[0]
        vz = jnp.zeros_like(vt2)
        pad = jnp.zeros((LANES - EXTRA_ROWS, tk), BF16)
        ones = jnp.where(lax.broadcasted_iota(jnp.int32, (EXTRA_ROWS, tk), 0) == 0, 1.0, 0.0).astype(BF16)
        vaug = [jnp.concatenate([jnp.where(head0, vt2, vz).astype(BF16), ones, pad], axis=0),
                jnp.concatenate([jnp.where(head0, vz, vt2).astype(BF16), ones, pad], axis=0)]
        half = tq // 2
        key_max = lambda s: jnp.max(s, axis=0, keepdims=True)
        corr = []
        outs = []
        if masked:
            scores = []
            for qa in (qa0, qa1):
                s_l = _dot(kaug[0:half], qa[:, 0:half])
                s_r = _dot(kaug, qa[:, half:tq])
                s_l = jnp.where(lax.broadcasted_iota(jnp.int32, s_l.shape, 0)
                                <= lax.broadcasted_iota(jnp.int32, s_l.shape, 1), s_l, NEG_BIG)
                s_r = jnp.where(lax.broadcasted_iota(jnp.int32, s_r.shape, 0) - half
                                <= lax.broadcasted_iota(jnp.int32, s_r.shape, 1), s_r, NEG_BIG)
                scores.append((s_l, s_r))
        else:
            scores = [_dot(kaug, qa0[...]), _dot(kaug, qa1[...])]
        for s, va, m_s in zip(scores, vaug, (m0_s, m1_s)):
            m_old = m_s[...]
            if masked:
                s_l, s_r = s
                m_new = jnp.maximum(m_old, jnp.concatenate([key_max(s_l), key_max(s_r)], axis=1))
                p_l = jnp.exp2(s_l - m_new[:, 0:half]).astype(BF16)
                p_r = jnp.exp2(s_r - m_new[:, half:tq]).astype(BF16)
                out = jnp.concatenate([_dot(va[:, 0:half], p_l), _dot(va, p_r)], axis=1)
            else:
                m_new = jnp.maximum(m_old, key_max(s))
                out = _dot(va, jnp.exp2(s - m_new).astype(BF16))
            m_s[...] = m_new
            corr.append(jnp.exp2(m_old - m_new))
            outs.append(out)
        l0_s[...] = l0_s[...] * corr[0] + outs[0][LANES:LANES + SUBLANES]
        l1_s[...] = l1_s[...] * corr[1] + outs[1][LANES:LANES + SUBLANES]
        acc[...] = acc[...] * jnp.where(head0, corr[0], corr[1]) + outs[0][:LANES] + outs[1][:LANES]

    @pl.when(ki < qi)
    def _():
        update(False)

    @pl.when(ki == qi)
    def _():
        update(True)
        ot = acc[...] * jnp.where(head0, 1.0 / l0_s[0:1, :], 1.0 / l1_s[0:1, :])
        o = ot.T
        o2 = o * o
        lane0 = lax.broadcasted_iota(jnp.int32, (1, LANES), 1) < HEAD_DIM
        ms0 = jnp.sum(jnp.where(lane0, o2, 0.0), -1, keepdims=True) * (1.0 / HEAD_DIM)
        ms1 = jnp.sum(jnp.where(lane0, 0.0, o2), -1, keepdims=True) * (1.0 / HEAD_DIM)
        rn = jnp.where(lane0, lax.rsqrt(ms0 + RMS_EPS), lax.rsqrt(ms1 + RMS_EPS))
        o_ref[0] = (o * rn * g_ref[...]).astype(BF16)


def _attn(qt, eqt, kr, ek, vt, gain, *, tq):
    nb, _, s = qt.shape
    tk = tq
    nq = s // tq
    pairs = D_FX // LANES
    qi_tab = jnp.asarray([qi for qi in range(nq) for _ in range(qi + 1)], jnp.int32)
    ki_tab = jnp.asarray([ki for qi in range(nq) for ki in range(qi + 1)], jnp.int32)
    q_col = lambda b, p, s_, qt_, kt_: (b, p, qt_[s_])
    k_row = lambda b, p, s_, qt_, kt_: (b, kt_[s_], p)
    return pl.pallas_call(
        functools.partial(_attn_kernel, tq=tq, tk=tk),
        grid_spec=pltpu.PrefetchScalarGridSpec(
            num_scalar_prefetch=2,
            grid=(nb, pairs, int(qi_tab.shape[0])),
            in_specs=[pl.BlockSpec((1, LANES, tq), q_col),
                      pl.BlockSpec((1, EXTRA_ROWS, tq), q_col),
                      pl.BlockSpec((1, tk, LANES), k_row),
                      pl.BlockSpec((1, tk, LANES), k_row),
                      pl.BlockSpec((1, LANES, tk), lambda b, p, s_, qt_, kt_: (b, p, kt_[s_])),
                      pl.BlockSpec((1, LANES), lambda b, p, s_, qt_, kt_: (0, p))],
            out_specs=pl.BlockSpec((1, tq, LANES), lambda b, p, s_, qt_, kt_: (b, qt_[s_], p)),
            scratch_shapes=[pltpu.VMEM((2 * LANES, tq), BF16), pltpu.VMEM((2 * LANES, tq), BF16),
                            pltpu.VMEM((1, tq), F32), pltpu.VMEM((1, tq), F32),
                            pltpu.VMEM((SUBLANES, tq), F32), pltpu.VMEM((SUBLANES, tq), F32),
                            pltpu.VMEM((LANES, tq), F32)]),
        out_shape=jax.ShapeDtypeStruct((nb, s, D_FX), BF16),
        compiler_params=_params(("parallel", "parallel", "arbitrary")),
        name="attn",
    )(qi_tab, ki_tab, qt, eqt, kr, ek, vt, gain)


def _dec_attn_kernel(pt_ref, q_ref, kn_ref, vn_ref, lfn_ref, gt_ref, hm_ref, g_ref, *rest, pc, nc, t_new):
    kp = rest[0:pc]
    vp = rest[pc:2 * pc]
    lp = rest[2 * pc:3 * pc]
    o_ref, m_s, l_s, acc, carry = rest[3 * pc:]
    c = pl.program_id(1)
    q = q_ref[0]
    per_query = lambda bias: jnp.concatenate([bias] * t_new, axis=0)

    def update(s, pv_of):
        m_old = m_s[...]
        m_new = jnp.maximum(m_old, jnp.max(s, -1, keepdims=True))
        corr = jnp.exp(m_old - m_new)
        p = jnp.exp(s - m_new)
        l_s[...] = l_s[...] * corr + jnp.sum(p, -1, keepdims=True)
        m_s[...] = m_new
        acc[...] = acc[...] * corr + pv_of(p.astype(BF16))

    @pl.when(c == 0)
    def _():
        m_s[...] = jnp.full_like(m_s, NEG_BIG)
        l_s[...] = jnp.zeros_like(l_s)
        acc[...] = jnp.zeros_like(acc)
        lf = lfn_ref[0]
        nk = kn_ref.shape[1]
        s = _dot_nt(q, kn_ref[0].astype(BF16)) + per_query(_rows_times(lf, gt_ref[...])[:, 0:nk])
        key_t = lax.broadcasted_iota(jnp.int32, s.shape, 1)
        qry_t = lax.broadcasted_iota(jnp.int32, s.shape, 0) // H_FX
        update(jnp.where(key_t <= qry_t, s, NEG_BIG), lambda p: _dot(p, vn_ref[0].astype(BF16)))
        carry[...] = jnp.sum(lf, -1, keepdims=True)

    lf_all = jnp.concatenate([lp[i][...] for i in range(pc)], axis=0)
    within = _rows_times(lf_all, gt_ref[...])
    totals = jnp.sum(lf_all, -1, keepdims=True)
    later = carry[...]
    scores = []
    for i in range(pc):
        rows = slice(i * H_FX, (i + 1) * H_FX)
        scores.append(_dot(q, kp[i][...].astype(BF16)) + per_query(within[rows] + later))
        later = later + totals[rows]
    carry[...] = later

    def pages_pv(p):
        out = _dot_nt(p[:, 0:PAGE_SIZE], vp[0][...].astype(BF16))
        for i in range(1, pc):
            out = out + _dot_nt(p[:, i * PAGE_SIZE:(i + 1) * PAGE_SIZE], vp[i][...].astype(BF16))
        return out

    update(jnp.concatenate(scores, axis=1), pages_pv)

    @pl.when(c == nc - 1)
    def _():
        o = acc[...] * (1.0 / l_s[...]) * hm_ref[...]
        ms = jnp.sum(o * o, -1, keepdims=True) * (1.0 / HEAD_DIM)
        o = o * lax.rsqrt(ms + RMS_EPS)
        per_t = [jnp.sum(o[t * H_FX:(t + 1) * H_FX], axis=0, keepdims=True) for t in range(t_new)]
        o_ref[0] = (jnp.concatenate(per_t, axis=0) * g_ref[...]).astype(BF16)


def _dec_attn(page_table, qbig, k_new, v_new, lf_new, cache_kt, cache_vt, cache_lft, head_mask, gain, *, pc, t_new):
    nb, n_pages = page_table.shape
    nc = n_pages // pc
    nk = k_new.shape[1]
    gt = jnp.triu(jnp.ones((PAGE_SIZE, PAGE_SIZE), BF16), 1).T
    rows = t_new * H_FX

    def page_spec(shape, i):
        return pl.BlockSpec((None,) + shape, lambda b, c, pt: (pt[b, n_pages - 1 - (c * pc + i)], 0, 0))

    per_b = lambda shape: pl.BlockSpec((1,) + shape, lambda b, c, pt: (b, 0, 0))
    const = lambda shape: pl.BlockSpec(shape, lambda b, c, pt: (0, 0))
    in_specs = ([per_b((rows, D_FX)), per_b((nk, D_FX)), per_b((nk, D_FX)), per_b((H_FX, LANES)),
                 const((PAGE_SIZE, PAGE_SIZE)), const((rows, D_FX)), const((1, D_FX))]
                + [page_spec((D_FX, PAGE_SIZE), i) for i in range(pc)]
                + [page_spec((D_FX, PAGE_SIZE), i) for i in range(pc)]
                + [page_spec((H_FX, PAGE_SIZE), i) for i in range(pc)])
    return pl.pallas_call(
        functools.partial(_dec_attn_kernel, pc=pc, nc=nc, t_new=t_new),
        grid_spec=pltpu.PrefetchScalarGridSpec(
            num_scalar_prefetch=1,
            grid=(nb, nc),
            in_specs=in_specs,
            out_specs=pl.BlockSpec((1, t_new, D_FX), lambda b, c, pt: (b, 0, 0)),
            scratch_shapes=[pltpu.VMEM((rows, 1), F32), pltpu.VMEM((rows, 1), F32),
                            pltpu.VMEM((rows, D_FX), F32), pltpu.VMEM((H_FX, 1), F32)]),
        out_shape=jax.ShapeDtypeStruct((nb, t_new, D_FX), BF16),
        compiler_params=_params(("parallel", "arbitrary")),
        name="dec_attn",
    )(page_table, qbig, k_new, v_new, lf_new, gt, head_mask, gain,
      *([cache_kt] * pc), *([cache_vt] * pc), *([cache_lft] * pc))


FF_CHUNK = D_FF // 2
_TAIL_CONSTS = ['ln_in_g', 'ln_in_b', 'rw_gn_g', 'rw_gn_b', 'bd', 'wo_rw', 'wo_fx', 'ln1_g', 'ln1_b',
                'w_up_a', 'w_up_b', 'conv_w', 'conv_b', 'w_down', 'ln2_g', 'ln2_b', 'w_pg', 'b_pg', 'w_ple']


def _tail_kernel(x_ref, wkv_ref, bonus_ref, g_ref, fx_ref, pe_ref, hist0_ref,
                 lng, lnb, gng, gnb, bd, wo_rw, wo_fx, l1g, l1b,
                 wa, wb, cw, cb, wd, l2g, l2b, wpg, bpg, wple, y_ref, hist_o, aext, *, tm, sh, hist):
    @pl.when(pl.program_id(1) == 0)
    def _():
        aext[0:hist, :] = hist0_ref[0]

    xn = _layer_norm(x_ref[0], lng[...], lnb[...])
    wkv = wkv_ref[0]
    mean = _segsum(wkv, bd[...]) * (1.0 / HEAD_DIM)
    wc = wkv - mean
    var = _segsum(wc * wc, bd[...]) * (1.0 / HEAD_DIM)
    wn = wc * lax.rsqrt(var + GN_EPS) * gng[...] + gnb[...]
    rw_y = (wn + bonus_ref[0]) * g_ref[0]
    mix = _dot(rw_y.astype(BF16), wo_rw[...]) + _dot(fx_ref[0], wo_fx[...])
    x1 = _layer_norm(ALPHA * xn + mix, l1g[...], l1b[...])
    xb = x1.astype(BF16)
    y = jnp.zeros((tm, D_MODEL), F32)
    for c0 in range(0, D_FF, FF_CHUNK):
        cs = slice(c0, c0 + FF_CHUNK)
        a = _dot(xb, wa[:, cs])
        aext[hist:hist + tm, cs] = a
        p1 = aext[hist - sh:hist - sh + tm, cs]
        p2 = aext[hist - 2 * sh:hist - 2 * sh + tm, cs]
        conv = cb[:, cs] + p2 * cw[0:1, cs] + p1 * cw[1:2, cs] + a * cw[2:3, cs]
        gelu = 0.5 * conv * (1.0 + lax.erf(conv * (2.0 ** -0.5)))
        h = gelu * _dot(xb, wb[:, cs])
        y = y + _dot(h.astype(BF16), wd[cs, :])
    new_hist = aext[tm:tm + hist, :]
    hist_o[0] = new_hist
    aext[0:hist, :] = new_hist
    x2 = _layer_norm(ALPHA * x1 + y, l2g[...], l2b[...])
    gate = jax.nn.sigmoid(_dot(x2.astype(BF16), wpg[...]) + bpg[...])
    y_ref[0] = x2 + gate * _dot(pe_ref[0].astype(BF16), wple[...])


def _tail(x, wkv, bonus, g, fx, pe, hist0, wts, *, tm, sh):
    nb, n, _ = x.shape
    hist = hist0.shape[1]
    consts = [wts[nm] for nm in _TAIL_CONSTS]
    row = lambda c: pl.BlockSpec((1, tm, c), lambda b, t: (b, t, 0))
    hist_spec = pl.BlockSpec((1, hist, D_FF), lambda b, t: (b, 0, 0))
    return pl.pallas_call(
        functools.partial(_tail_kernel, tm=tm, sh=sh, hist=hist),
        grid=(nb, n // tm),
        in_specs=[row(D_MODEL), row(D_RW), row(D_RW), row(D_RW), row(D_FX), row(PLE_DIM), hist_spec]
                 + [_const_spec(c.shape) for c in consts],
        out_specs=[row(D_MODEL), hist_spec],
        out_shape=[jax.ShapeDtypeStruct((nb, n, D_MODEL), F32), jax.ShapeDtypeStruct((nb, hist, D_FF), F32)],
        scratch_shapes=[pltpu.VMEM((hist + tm, D_FF), F32)],
        compiler_params=_params(("parallel", "arbitrary")),
        name="tail",
    )(x, wkv, bonus, g, fx, pe, hist0, *consts)


def _prep_weights(ln_in_g, ln_in_b, w_in, b_f, rw_mu, rw_w0, rw_w2, rw_a0, rw_a2, rw_g2, rw_kk, rw_ka, rw_rk,
                  rw_gn_g, rw_gn_b, fx_norm_g, w_o, ln1_g, ln1_b, w_up, conv_w, conv_b, w_down, ln2_g, ln2_b,
                  w_ple, w_pg, b_pg):
    r2 = lambda a: a.reshape(1, -1)
    bf = lambda a: a.astype(BF16)
    lora_rows = R_DECAY + R_ICL
    seg = jnp.arange(D_RW) // HEAD_DIM
    fx0 = C_RW
    w_f = w_in[:, fx0 + 3 * D_FX:]
    place_q, ones_q, place_k, ones_k = _bias_placement()
    return dict(
        ln_in_g=r2(ln_in_g), ln_in_b=r2(ln_in_b),
        w_rw=bf(w_in[:, :C_RW]),
        w_q=bf(w_in[:, fx0:fx0 + D_FX]),
        w_k=bf(w_in[:, fx0 + D_FX:fx0 + 2 * D_FX]), w_v=bf(w_in[:, fx0 + 2 * D_FX:fx0 + 3 * D_FX]),
        w_qt=bf(w_in[:, fx0:fx0 + D_FX].T), w_kt=bf(w_in[:, fx0 + D_FX:fx0 + 2 * D_FX].T), w_vt=bf(w_in[:, fx0 + 2 * D_FX:fx0 + 3 * D_FX].T),
        w_f=bf(jnp.pad(w_f, ((0, 0), (0, LANES - H_FX)))),
        b_f=jnp.pad(b_f, (0, LANES - H_FX)).reshape(1, LANES),
        w_ft=bf(jnp.pad(w_f.T, ((0, PACKED_ROWS - H_FX), (0, 0)))), b_ft=b_f.reshape(H_FX, 1),
        place_q=place_q, ones_q=ones_q, place_k=place_k, ones_k=ones_k,
        rw_mu=r2(rw_mu), rw_w0=r2(rw_w0), rw_a0=r2(rw_a0),
        rw_w2=bf(jnp.pad(rw_w2, ((0, lora_rows - R_DECAY), (0, 0)))),
        rw_a2=bf(jnp.pad(rw_a2, ((lora_rows - R_ICL, 0), (0, 0)))),
        rw_g2=bf(rw_g2), rw_kk=r2(rw_kk), rw_ka=r2(rw_ka), rw_rk=r2(rw_rk),
        bd=(seg[:, None] == seg[None, :]).astype(BF16),
        rw_gn_g=r2(rw_gn_g), rw_gn_b=r2(rw_gn_b), fx_norm_g=r2(fx_norm_g),
        wo_rw=bf(w_o[:D_RW]), wo_fx=bf(w_o[D_RW:]), ln1_g=r2(ln1_g), ln1_b=r2(ln1_b),
        w_up_a=bf(w_up[:, :D_FF]), w_up_b=bf(w_up[:, D_FF:]), conv_w=conv_w, conv_b=r2(conv_b),
        w_down=bf(w_down), ln2_g=r2(ln2_g), ln2_b=r2(ln2_b), w_ple=bf(w_ple), w_pg=bf(w_pg), b_pg=r2(b_pg))


def _prompt_front(x, wts):
    nb = x.shape[0]
    outs = _proj(x, jnp.zeros((nb, SUBLANES, C_RW), F32), wts, tm=512, sh=1, prompt=True)
    (qt, eqt, kr, ek, _, vt) = outs[8:14]
    return outs, _attn(qt, eqt, kr, ek, vt, wts['fx_norm_g'], tq=1024)


def _prompt_back(x, pe, proj_outs, fx, wts, after):
    nb, s, _ = x.shape
    hist = SUBLANES
    (r, d, k, v, nkk, ka, g, bonus, _, _, _, _, kt, vt, logft, shift_hist) = proj_outs
    to_scan = lambda a: a.reshape(nb, s, H_RW, HEAD_DIM).transpose(1, 3, 0, 2).reshape(s, 1, HEAD_DIM, nb * H_RW)
    xs, fx, _ = lax.optimization_barrier(([to_scan(a) for a in (r, d, k, nkk, ka, v)], fx, after))
    wkv, s_fin = _scan(xs, jnp.zeros((1, HEAD_DIM, HEAD_DIM, nb * H_RW), F32), tt=16)
    wkv = wkv.reshape(s, HEAD_DIM, nb, H_RW).transpose(2, 0, 3, 1).reshape(nb, s, D_RW)
    s_fin = s_fin.reshape(HEAD_DIM, HEAD_DIM, nb, H_RW).transpose(2, 3, 1, 0)
    y, conv_hist = _tail(x, wkv, bonus, g, fx, pe, jnp.zeros((nb, hist, D_FF), F32), wts, tm=512, sh=1)
    heads = lambda a: a.reshape(nb, H_FX, HEAD_DIM, s).transpose(0, 3, 1, 2)
    return y, (heads(kt), heads(vt), logft.transpose(0, 2, 1), s_fin, shift_hist[:, hist - 1],
               conv_hist[:, hist - (CONV_W - 1):])


def _sample_front(x, state_shift, cache_k, cache_v, cache_logf, page_table, wts):
    nb, t_new, _ = x.shape
    n = nb * t_new
    tmajor = lambda a: a.transpose(1, 0, 2).reshape(1, n, a.shape[-1])
    bmajor = lambda a: a.reshape(t_new, nb, a.shape[-1]).transpose(1, 0, 2)
    outs = _proj(tmajor(x), state_shift.reshape(1, nb, C_RW), wts, tm=nb, sh=nb, prompt=False)
    q, kf, vf, logf = outs[8:12]
    kf_b, vf_b, logf_b = bmajor(kf), bmajor(vf), bmajor(logf)
    head_mask = (jnp.arange(D_FX)[None, :] // HEAD_DIM == jnp.arange(H_FX)[:, None]).astype(BF16)
    qbig = (bmajor(q)[:, :, None, :] * head_mask).reshape(nb, t_new * H_FX, D_FX)
    nk = PACKED_ROWS
    pad_rows = lambda a: jnp.pad(a, ((0, 0), (0, nk - t_new), (0, 0)))
    lf_new = jnp.pad(logf_b.transpose(0, 2, 1), ((0, 0), (0, 0), (0, LANES - t_new)))
    n_pool = cache_k.shape[0]
    pages_t = lambda c: c.transpose(0, 2, 3, 1).reshape(n_pool, D_FX, PAGE_SIZE)
    fx = _dec_attn(page_table, qbig, pad_rows(kf_b), pad_rows(vf_b), lf_new,
                   pages_t(cache_k), pages_t(cache_v), cache_logf.transpose(0, 2, 1),
                   jnp.tile(head_mask.astype(F32), (t_new, 1)), wts['fx_norm_g'], pc=32, t_new=t_new)
    return outs, (kf_b, vf_b, logf_b), fx


def _sample_back(x, pe, proj_outs, new_kv, fx, state_wkv, state_conv, wts):
    nb, t_new, _ = x.shape
    n = nb * t_new
    tmajor = lambda a: a.transpose(1, 0, 2).reshape(1, n, a.shape[-1])
    bmajor = lambda a: a.reshape(t_new, nb, a.shape[-1]).transpose(1, 0, 2)
    x_tm = tmajor(x)
    (r, d, k, v, nkk, ka, g, bonus, _, _, _, _, shift_hist) = proj_outs
    kf_b, vf_b, logf_b = new_kv
    to_scan = lambda a: a.reshape(t_new, nb, H_RW, HEAD_DIM).transpose(0, 2, 3, 1)
    wkv, s_fin = _scan([to_scan(a) for a in (r, d, k, nkk, ka, v)], state_wkv.transpose(1, 3, 2, 0), tt=t_new)
    wkv = wkv.transpose(0, 3, 1, 2).reshape(1, n, D_RW)
    s_fin = s_fin.transpose(3, 0, 2, 1)
    conv0 = state_conv.transpose(1, 0, 2).reshape(1, (CONV_W - 1) * nb, D_FF)
    y, conv_hist = _tail(x_tm, wkv, bonus, g, tmajor(fx), tmajor(pe), conv0, wts, tm=nb, sh=nb)
    heads = lambda a: a.reshape(nb, t_new, H_FX, HEAD_DIM)
    conv_new = conv_hist.reshape(CONV_W - 1, nb, D_FF).transpose(1, 0, 2)
    return bmajor(y[0]), (heads(kf_b), heads(vf_b), logf_b, s_fin, shift_hist[0], conv_new)


def kernel(x_prompt, x_sample, cache_k, cache_v, cache_logf, state_wkv, state_shift, state_conv, page_table, p_prompt, p_sample, ln_in_g, ln_in_b, w_in, b_f, rw_mu, rw_w0, rw_w2, rw_a0, rw_a2, rw_g2, rw_kk, rw_ka, rw_rk, rw_gn_g, rw_gn_b, fx_norm_g, w_o, ln1_g, ln1_b, w_up, conv_w, conv_b, w_down, ln2_g, ln2_b, w_ple, w_pg, b_pg):
    assert w_in.shape[0] == 1, "single-layer model"
    wts = _prep_weights(ln_in_g, ln_in_b, w_in[0], b_f[0], rw_mu[0], rw_w0[0], rw_w2[0], rw_a0[0], rw_a2[0],
                        rw_g2[0], rw_kk[0], rw_ka[0], rw_rk[0].reshape(-1), rw_gn_g[0], rw_gn_b[0], fx_norm_g[0],
                        w_o[0], ln1_g[0], ln1_b[0], w_up[0], conv_w[0], conv_b[0], w_down[0], ln2_g[0], ln2_b[0],
                        w_ple[0], w_pg[0], b_pg[0])
    proj_p, fx_p = _prompt_front(x_prompt, wts)
    proj_s, new_kv, fx_s = _sample_front(x_sample, state_shift[0], cache_k[0], cache_v[0], cache_logf[0],
                                         page_table, wts)
    yp, st_p = _prompt_back(x_prompt, p_prompt[0], proj_p, fx_p, wts, after=fx_s)
    ys, st_s = _sample_back(x_sample, p_sample[0], proj_s, new_kv, fx_s, state_wkv[0], state_conv[0], wts)
    return (yp, ys) + tuple(a[None] for a in st_p) + tuple(a[None] for a in st_s)
```
